```python
import math
import jax, jax.numpy as jnp
from jax import lax
import numpy as np

D_MODEL = 1024
BATCH = 2
SEQ = 8192
DEPTH = 4

N_MIXERS = 2
Q_BLOCK = 128
EPS = 1e-6
POS_OFFSET_MAX = 4096

MLA_HEADS = 16
MLA_NOPE = 64
MLA_ROPE = 32
MLA_V = 64
MLA_Q_RANK = 384
MLA_KV_RANK = 256
ROPE_BASE = 10000.0
MLA_GATE = MLA_HEADS * MLA_V
MLA_IN = MLA_Q_RANK + MLA_KV_RANK + MLA_ROPE + MLA_GATE

DIFF_HD = 64
DIFF_HEADS = D_MODEL // (2 * DIFF_HD)
DIFF_W = DIFF_HEADS * 2 * DIFF_HD
DIFF_IN = 4 * DIFF_W

N_MLA = (DEPTH + 1) // 2
N_DIFF = DEPTH // 2

kernel_name = "hybrid_mla_diffattn_adaln_encoder"


def rmsnorm(x, g):
    xf = x.astype(jnp.float32)
    y = xf * lax.rsqrt(jnp.mean(xf * xf, axis=-1, keepdims=True) + EPS)
    return (y * g.astype(jnp.float32)).astype(x.dtype)


def blockwise(fn, *arrs):
    b, s = arrs[0].shape[:2]
    nb = s // Q_BLOCK
    blocks = tuple(a.reshape((b, nb, Q_BLOCK) + a.shape[2:]).swapaxes(0, 1) for a in arrs)
    out = lax.map(fn, blocks)
    return out.swapaxes(0, 1).reshape((b, s) + out.shape[3:])


def rope_tables(positions):
    inv = ROPE_BASE ** (-jnp.arange(0, MLA_ROPE, 2, dtype=jnp.float32) / MLA_ROPE)
    ang = positions.astype(jnp.float32)[..., None] * inv
    return jnp.cos(ang), jnp.sin(ang)


def apply_rope(x, cos, sin):
    half = x.shape[-1] // 2
    x1, x2 = x[..., :half], x[..., half:]
    cos = cos.astype(x.dtype)
    sin = sin.astype(x.dtype)
    return jnp.concatenate([x1 * cos - x2 * sin, x1 * sin + x2 * cos], axis=-1)


def mla_mixer(h, positions, w_in, q_norm_g, w_q_up, kv_norm_g, w_kv_up, w_o):
    b, s, _ = h.shape
    proj = h @ w_in
    o1 = MLA_Q_RANK
    o2 = o1 + MLA_KV_RANK
    o3 = o2 + MLA_ROPE
    q_lat, kv_lat, k_rope, gate = proj[..., :o1], proj[..., o1:o2], proj[..., o2:o3], proj[..., o3:]
    q = (rmsnorm(q_lat, q_norm_g) @ w_q_up).reshape(b, s, MLA_HEADS, MLA_NOPE + MLA_ROPE)
    kv = (rmsnorm(kv_lat, kv_norm_g) @ w_kv_up).reshape(b, s, MLA_HEADS, MLA_NOPE + MLA_V)
    k_nope, v = kv[..., :MLA_NOPE], kv[..., MLA_NOPE:]
    cos, sin = rope_tables(positions)
    scale = (MLA_NOPE + MLA_ROPE) ** -0.5
    q_nope = q[..., :MLA_NOPE] * scale
    q_rope = apply_rope(q[..., MLA_NOPE:], cos[:, :, None], sin[:, :, None]) * scale
    k_rope = apply_rope(k_rope, cos, sin)

    def block(args):
        qn, qr = args
        sc = (jnp.einsum('bqhd,bkhd->bhqk', qn, k_nope)
              + jnp.einsum('bqhd,bkd->bhqk', qr, k_rope))
        p = jax.nn.softmax(sc.astype(jnp.float32), axis=-1)
        return jnp.einsum('bhqk,bkhd->bqhd', p.astype(v.dtype), v)

    o = blockwise(block, q_nope, q_rope)
    return (o.reshape(b, s, MLA_GATE) * jax.nn.silu(gate)) @ w_o


def diff_mixer(h, positions, w_in, lq1, lk1, lq2, lk2, head_g, w_o, lambda_init):
    b, s, _ = h.shape
    proj = h @ w_in
    q, k, v, gate = (proj[..., i * DIFF_W:(i + 1) * DIFF_W] for i in range(4))
    q = q.reshape(b, s, DIFF_HEADS, 2, DIFF_HD) * (DIFF_HD ** -0.5)
    k = k.reshape(b, s, DIFF_HEADS, 2, DIFF_HD)
    v = v.reshape(b, s, DIFF_HEADS, 2 * DIFF_HD)
    lam = (jnp.exp(jnp.sum(lq1.astype(jnp.float32) * lk1.astype(jnp.float32)))
           - jnp.exp(jnp.sum(lq2.astype(jnp.float32) * lk2.astype(jnp.float32)))
           + lambda_init)
    slopes = jnp.exp2(-8.0 * jnp.arange(1, DIFF_HEADS + 1, dtype=jnp.float32) / DIFF_HEADS)

    def block(args):
        qb, pb = args
        sc = jnp.einsum('bqhcd,bkhcd->bhcqk', qb, k).astype(jnp.float32)
        dist = jnp.abs(pb[:, :, None] - positions[:, None, :]).astype(jnp.float32)
        bias = -slopes[None, :, None, None, None] * dist[:, None, None]
        p = jax.nn.softmax(sc + bias, axis=-1)
        a = p[:, :, 0] - lam * p[:, :, 1]
        return jnp.einsum('bhqk,bkhd->bqhd', a.astype(v.dtype), v)

    o = blockwise(block, q, positions)
    o = rmsnorm(o, head_g) * (1.0 - lambda_init)
    return (o.reshape(b, s, DIFF_W) * jax.nn.silu(gate)) @ w_o


def setup_inputs(seed: int = 0) -> dict:
    key = jax.random.key(seed)
    ks = jax.random.split(key, 24)
    f32 = jnp.float32
    D = D_MODEL

    def nrm(k, shape, fan_in):
        return jax.random.normal(k, shape, f32) * (fan_in ** -0.5)

    def gain(k, shape):
        return 1.0 + 0.02 * jax.random.normal(k, shape, f32)

    x = jax.random.normal(ks[0], (BATCH, SEQ, D), f32)
    c = jax.random.normal(ks[1], (BATCH, D), f32)
    offs = jax.random.randint(ks[2], (BATCH, 1), 0, POS_OFFSET_MAX, dtype=jnp.int32)
    positions = (offs + jnp.arange(SEQ, dtype=jnp.int32)[None, :]).astype(jnp.int32)
    return {
        "x": x,
        "c": c,
        "positions": positions,
        "ada_w": nrm(ks[3], (DEPTH, D, 3 * D), D),
        "ada_b": 0.02 * jax.random.normal(ks[4], (DEPTH, 3 * D), f32),
        "norm_g": gain(ks[5], (DEPTH, D)),
        "mla_w_in": nrm(ks[6], (N_MLA, D, MLA_IN), D),
        "mla_q_norm_g": gain(ks[7], (N_MLA, MLA_Q_RANK)),
        "mla_w_q_up": nrm(ks[8], (N_MLA, MLA_Q_RANK, MLA_HEADS * (MLA_NOPE + MLA_ROPE)), MLA_Q_RANK),
        "mla_kv_norm_g": gain(ks[9], (N_MLA, MLA_KV_RANK)),
        "mla_w_kv_up": nrm(ks[10], (N_MLA, MLA_KV_RANK, MLA_HEADS * (MLA_NOPE + MLA_V)), MLA_KV_RANK),
        "mla_w_o": nrm(ks[11], (N_MLA, MLA_GATE, D), MLA_GATE),
        "diff_w_in": nrm(ks[12], (N_DIFF, D, DIFF_IN), D),
        "diff_lq1": 0.1 * jax.random.normal(ks[13], (N_DIFF, DIFF_HD), f32),
        "diff_lk1": 0.1 * jax.random.normal(ks[14], (N_DIFF, DIFF_HD), f32),
        "diff_lq2": 0.1 * jax.random.normal(ks[15], (N_DIFF, DIFF_HD), f32),
        "diff_lk2": 0.1 * jax.random.normal(ks[16], (N_DIFF, DIFF_HD), f32),
        "diff_head_g": gain(ks[17], (N_DIFF, 2 * DIFF_HD)),
        "diff_w_o": nrm(ks[18], (N_DIFF, DIFF_W, D), DIFF_W),
        "final_g": gain(ks[19], (D,)),
    }


def reference(x, c, positions, ada_w, ada_b, norm_g,
              mla_w_in, mla_q_norm_g, mla_w_q_up, mla_kv_norm_g, mla_w_kv_up, mla_w_o,
              diff_w_in, diff_lq1, diff_lk1, diff_lq2, diff_lk2, diff_head_g, diff_w_o,
              final_g):
    c_act = jax.nn.silu(c)
    for i in range(DEPTH):
        mod = c_act @ ada_w[i] + ada_b[i]
        shift, scale, gate = mod[:, :D_MODEL], mod[:, D_MODEL:2 * D_MODEL], mod[:, 2 * D_MODEL:]
        h = rmsnorm(x, norm_g[i]) * (1.0 + scale[:, None, :]) + shift[:, None, :]
        j = i // N_MIXERS
        if i % N_MIXERS == 0:
            y = mla_mixer(h, positions, mla_w_in[j], mla_q_norm_g[j], mla_w_q_up[j],
                          mla_kv_norm_g[j], mla_w_kv_up[j], mla_w_o[j])
        else:
            lambda_init = 0.8 - 0.6 * math.exp(-0.3 * i)
            y = diff_mixer(h, positions, diff_w_in[j], diff_lq1[j], diff_lk1[j],
                           diff_lq2[j], diff_lk2[j], diff_head_g[j], diff_w_o[j], lambda_init)
        x = x + gate[:, None, :] * y
    return rmsnorm(x, final_g)
```

```python
import functools
import math

import jax
import jax.numpy as jnp
from jax import lax
from jax.experimental import pallas as pl
from jax.experimental.pallas import tpu as pltpu

D_MODEL = 1024
DEPTH = 4
EPS = 1e-6

MLA_HEADS = 16
MLA_NOPE = 64
MLA_ROPE = 32
MLA_V = 64
MLA_Q_RANK = 384
MLA_KV_RANK = 256
ROPE_BASE = 10000.0
MLA_QK = MLA_NOPE + MLA_ROPE

DIFF_HD = 64
DIFF_HEADS = D_MODEL // (2 * DIFF_HD)
DIFF_W = DIFF_HEADS * 2 * DIFF_HD

LANES = 128
HEAD_PAD = LANES
LOG2E = math.log2(math.e)

ROW_TILE = 256
Q_TILE = 512
K_TILE = 512
VMEM_LIMIT = 56 * 1024 * 1024

BF16 = jnp.bfloat16
F32 = jnp.float32


def _cparams(sem):
    return pltpu.CompilerParams(dimension_semantics=sem, vmem_limit_bytes=VMEM_LIMIT)


def _dot(a, b):
    return jnp.dot(a, b, preferred_element_type=F32)


def _rms(x, g):
    return x * lax.rsqrt(jnp.mean(x * x, axis=-1, keepdims=True) + EPS) * g


def _silu(x):
    return x * (1.0 / (1.0 + jnp.exp(-x)))


def _mod_kernel(c_ref, w_ref, b_ref, o_ref):
    c = c_ref[...]
    o_ref[0] = _dot(_silu(c).astype(BF16), w_ref[0].astype(BF16)) + b_ref[0]


def _ada_mod(c_pad, ada_w, ada_b):
    rows = c_pad.shape[0]
    nblk = 3
    return pl.pallas_call(
        _mod_kernel,
        out_shape=jax.ShapeDtypeStruct((DEPTH, rows, 3 * D_MODEL), F32),
        grid=(DEPTH, nblk),
        in_specs=[
            pl.BlockSpec((rows, D_MODEL), lambda i, n: (0, 0)),
            pl.BlockSpec((1, D_MODEL, D_MODEL), lambda i, n: (i, 0, n)),
            pl.BlockSpec((1, 1, D_MODEL), lambda i, n: (i, 0, n)),
        ],
        out_specs=pl.BlockSpec((1, rows, D_MODEL), lambda i, n: (i, 0, n)),
        compiler_params=_cparams(("arbitrary", "arbitrary")),
        name="ada_mod",
    )(c_pad, ada_w, ada_b.reshape(DEPTH, 1, 3 * D_MODEL))


def _modulated_norm(x_ref, mod_ref, g_ref):
    x = x_ref[0]
    shift = mod_ref[0, :, 0:D_MODEL]
    scale = mod_ref[0, :, D_MODEL:2 * D_MODEL]
    return _rms(x, g_ref[...]) * (1.0 + scale) + shift


def _mla_pre_kernel(x_ref, mod_ref, g_ref, pos_ref, invf_ref, win_ref, qg_ref, wq_ref,
                    kvg_ref, wkv_ref, q_out, kt_out, v_out, sg_out):
    h = _modulated_norm(x_ref, mod_ref, g_ref).astype(BF16)
    proj = _dot(h, win_ref[...])
    o1 = MLA_Q_RANK
    o2 = o1 + MLA_KV_RANK
    o3 = o2 + HEAD_PAD
    o4 = o3 + HEAD_PAD
    qn = _rms(proj[:, :o1], qg_ref[...]).astype(BF16)
    kvn = _rms(proj[:, o1:o2], kvg_ref[...]).astype(BF16)
    ang = pos_ref[0].astype(F32) * invf_ref[...]
    cos = jnp.cos(ang)
    sin = jnp.sin(ang)
    k_rope = proj[:, o2:o3] * cos + proj[:, o3:o4] * sin
    sg_out[0] = _silu(proj[:, o4:])

    qq = _dot(qn, wq_ref[...])
    kv = _dot(kvn, wkv_ref[...])
    half = MLA_HEADS * HEAD_PAD
    qscale = (MLA_QK ** -0.5) * LOG2E
    lane = lax.broadcasted_iota(jnp.int32, (1, HEAD_PAD), 1)
    ones_col = (lane == MLA_V).astype(F32)
    for hd in range(MLA_HEADS):
        a = hd * HEAD_PAD
        b = a + HEAD_PAD
        q_h = (qq[:, a:b] * cos + qq[:, half + a:half + b] * sin) * qscale
        q_out[0, hd] = q_h.astype(BF16)
        k_h = kv[:, a:b] + k_rope
        kt_out[0, hd] = k_h.T.astype(BF16)
        v_out[0, hd] = (kv[:, half + a:half + b] + ones_col).astype(BF16)


def _mla_pre(x, mod_i, g, pos_col, invf, win, qg, wq, kvg, wkv):
    B, S, _ = x.shape
    T = ROW_TILE
    const = lambda shape: pl.BlockSpec(shape, lambda b, t: (0,) * len(shape))
    head_shape = jax.ShapeDtypeStruct((B, MLA_HEADS, S, HEAD_PAD), BF16)
    return pl.pallas_call(
        _mla_pre_kernel,
        out_shape=(
            head_shape,
            jax.ShapeDtypeStruct((B, MLA_HEADS, HEAD_PAD, S), BF16),
            head_shape,
            jax.ShapeDtypeStruct((B, S, D_MODEL), F32),
        ),
        grid=(B, S // T),
        in_specs=[
            pl.BlockSpec((1, T, D_MODEL), lambda b, t: (b, t, 0)),
            pl.BlockSpec((1, 1, 3 * D_MODEL), lambda b, t: (b, 0, 0)),
            const((1, D_MODEL)),
            pl.BlockSpec((1, T, 1), lambda b, t: (b, t, 0)),
            const((1, HEAD_PAD)),
            const(win.shape),
            const((1, MLA_Q_RANK)),
            const(wq.shape),
            const((1, MLA_KV_RANK)),
            const(wkv.shape),
        ],
        out_specs=(
            pl.BlockSpec((1, MLA_HEADS, T, HEAD_PAD), lambda b, t: (b, 0, t, 0)),
            pl.BlockSpec((1, MLA_HEADS, HEAD_PAD, T), lambda b, t: (b, 0, 0, t)),
            pl.BlockSpec((1, MLA_HEADS, T, HEAD_PAD), lambda b, t: (b, 0, t, 0)),
            pl.BlockSpec((1, T, D_MODEL), lambda b, t: (b, t, 0)),
        ),
        compiler_params=_cparams(("parallel", "parallel")),
        name="mla_pre",
    )(x, mod_i, g, pos_col, invf, win, qg, wq, kvg, wkv)


MLA_HEADS_PER_STEP = 2


def _mla_attn_kernel(q_ref, kt_ref, v_ref, sg_ref, o_ref, m_ref, acc_ref):
    n_kv = kt_ref.shape[-1] // K_TILE
    m_ref[...] = jnp.full(m_ref.shape, -jnp.inf, F32)
    acc_ref[...] = jnp.zeros(acc_ref.shape, F32)

    def body(j, carry):
        k0 = pl.multiple_of(j * K_TILE, K_TILE)
        for hh in range(MLA_HEADS_PER_STEP):
            s = _dot(q_ref[0, hh], kt_ref[0, hh, :, pl.ds(k0, K_TILE)])
            m_prev = m_ref[hh]
            m_new = jnp.maximum(m_prev, jnp.max(s, axis=-1, keepdims=True))
            p = jnp.exp2(s - m_new)
            alpha = jnp.exp2(m_prev - m_new)
            m_ref[hh] = m_new
            pv = _dot(p.astype(BF16), v_ref[0, hh, pl.ds(k0, K_TILE), :])
            acc_ref[hh] = alpha * acc_ref[hh] + pv
        return carry

    lax.fori_loop(0, n_kv, body, 0)

    outs = []
    for hh in range(MLA_HEADS_PER_STEP):
        acc = acc_ref[hh]
        outs.append(acc[:, :MLA_V] / acc[:, MLA_V:MLA_V + 1])
    o = jnp.concatenate(outs, axis=-1)
    o_ref[0] = (o * sg_ref[0]).astype(BF16)


def _mla_attn(q, kt, v, sg):
    B, H, S, _ = q.shape
    G = MLA_HEADS_PER_STEP
    return pl.pallas_call(
        _mla_attn_kernel,
        out_shape=jax.ShapeDtypeStruct((B, S, H * MLA_V), BF16),
        grid=(B, H // G, S // Q_TILE),
        in_specs=[
            pl.BlockSpec((1, G, Q_TILE, HEAD_PAD), lambda b, h, i: (b, h, i, 0)),
            pl.BlockSpec((1, G, HEAD_PAD, S), lambda b, h, i: (b, h, 0, 0)),
            pl.BlockSpec((1, G, S, HEAD_PAD), lambda b, h, i: (b, h, 0, 0)),
            pl.BlockSpec((1, Q_TILE, G * MLA_V), lambda b, h, i: (b, i, h)),
        ],
        out_specs=pl.BlockSpec((1, Q_TILE, G * MLA_V), lambda b, h, i: (b, i, h)),
        scratch_shapes=[
            pltpu.VMEM((G, Q_TILE, 1), F32),
            pltpu.VMEM((G, Q_TILE, HEAD_PAD), F32),
        ],
        compiler_params=_cparams(("parallel", "parallel", "parallel")),
        name="mla_attn",
    )(q, kt, v, sg)


def _diff_pre_kernel(x_ref, mod_ref, g_ref, win_ref, q_out, kt_out, v_out, sg_out):
    h = _modulated_norm(x_ref, mod_ref, g_ref).astype(BF16)
    proj = _dot(h, win_ref[...])
    qscale = (DIFF_HD ** -0.5) * LOG2E
    sg_out[0] = _silu(proj[:, 3 * DIFF_W:])
    for hd in range(DIFF_HEADS):
        a = hd * HEAD_PAD
        b = a + HEAD_PAD
        q_out[0, hd] = (proj[:, a:b] * qscale).astype(BF16)
        kt_out[0, hd] = proj[:, DIFF_W + a:DIFF_W + b].T.astype(BF16)
        v_out[0, hd] = proj[:, 2 * DIFF_W + a:2 * DIFF_W + b].astype(BF16)


def _diff_pre(x, mod_i, g, win):
    B, S, _ = x.shape
    T = ROW_TILE
    const = lambda shape: pl.BlockSpec(shape, lambda b, t: (0,) * len(shape))
    head_shape = jax.ShapeDtypeStruct((B, DIFF_HEADS, S, HEAD_PAD), BF16)
    return pl.pallas_call(
        _diff_pre_kernel,
        out_shape=(
            head_shape,
            jax.ShapeDtypeStruct((B, DIFF_HEADS, HEAD_PAD, S), BF16),
            head_shape,
            jax.ShapeDtypeStruct((B, S, D_MODEL), F32),
        ),
        grid=(B, S // T),
        in_specs=[
            pl.BlockSpec((1, T, D_MODEL), lambda b, t: (b, t, 0)),
            pl.BlockSpec((1, 1, 3 * D_MODEL), lambda b, t: (b, 0, 0)),
            const((1, D_MODEL)),
            const(win.shape),
        ],
        out_specs=(
            pl.BlockSpec((1, DIFF_HEADS, T, HEAD_PAD), lambda b, t: (b, 0, t, 0)),
            pl.BlockSpec((1, DIFF_HEADS, HEAD_PAD, T), lambda b, t: (b, 0, 0, t)),
            pl.BlockSpec((1, DIFF_HEADS, T, HEAD_PAD), lambda b, t: (b, 0, t, 0)),
            pl.BlockSpec((1, T, D_MODEL), lambda b, t: (b, t, 0)),
        ),
        compiler_params=_cparams(("parallel", "parallel")),
        name="diff_pre",
    )(x, mod_i, g, win)


def _diff_attn_kernel(q_ref, kt_ref, v_ref, sg_ref, posq_ref, posk_ref, slope_ref,
                      lq1_ref, lk1_ref, lq2_ref, lk2_ref, hg_ref, o_ref,
                      m_ref, l_ref, acc_ref, *, lambda_init):
    n_kv = kt_ref.shape[-1] // K_TILE
    m_ref[...] = jnp.full(m_ref.shape, -jnp.inf, F32)
    l_ref[...] = jnp.zeros(l_ref.shape, F32)
    acc_ref[...] = jnp.zeros(acc_ref.shape, F32)

    q = q_ref[0, 0]
    lane = lax.broadcasted_iota(jnp.int32, q.shape, 1)
    zero = jnp.zeros_like(q)
    q_maps = (jnp.where(lane < DIFF_HD, q, zero), jnp.where(lane >= DIFF_HD, q, zero))
    pos_q = posq_ref[0].astype(F32)
    slope = slope_ref[0]

    def body(j, carry):
        k0 = pl.multiple_of(j * K_TILE, K_TILE)
        kt = kt_ref[0, 0, :, pl.ds(k0, K_TILE)]
        vt = v_ref[0, 0, pl.ds(k0, K_TILE), :]
        pos_k = posk_ref[0, :, pl.ds(k0, K_TILE)].astype(F32)
        bias = jnp.abs(pos_q - pos_k) * slope[:, 0:1]
        for c in range(2):
            s = _dot(q_maps[c], kt) - bias
            m_prev = m_ref[c]
            m_new = jnp.maximum(m_prev, jnp.max(s, axis=-1, keepdims=True))
            p = jnp.exp2(s - m_new)
            alpha = jnp.exp2(m_prev - m_new)
            m_ref[c] = m_new
            l_ref[c] = alpha * l_ref[c] + jnp.sum(p, axis=-1, keepdims=True)
            acc_ref[c] = alpha * acc_ref[c] + _dot(p.astype(BF16), vt)
        return carry

    lax.fori_loop(0, n_kv, body, 0)

    lam = (jnp.exp(jnp.sum(lq1_ref[...] * lk1_ref[...], axis=-1, keepdims=True))
           - jnp.exp(jnp.sum(lq2_ref[...] * lk2_ref[...], axis=-1, keepdims=True))
           + lambda_init)
    o = acc_ref[0] / l_ref[0] - lam * (acc_ref[1] / l_ref[1])
    o = _rms(o, hg_ref[...]) * (1.0 - lambda_init)
    o_ref[0] = (o * sg_ref[0]).astype(BF16)


def _diff_attn(q, kt, v, sg, pos_col, pos_row, slopes, lq1, lk1, lq2, lk2, hg, lambda_init):
    B, H, S, _ = q.shape
    vec = lambda n: pl.BlockSpec((1, n), lambda b, h, i: (0, 0))
    return pl.pallas_call(
        functools.partial(_diff_attn_kernel, lambda_init=lambda_init),
        out_shape=jax.ShapeDtypeStruct((B, S, DIFF_W), BF16),
        grid=(B, H, S // Q_TILE),
        in_specs=[
            pl.BlockSpec((1, 1, Q_TILE, HEAD_PAD), lambda b, h, i: (b, h, i, 0)),
            pl.BlockSpec((1, 1, HEAD_PAD, S), lambda b, h, i: (b, h, 0, 0)),
            pl.BlockSpec((1, 1, S, HEAD_PAD), lambda b, h, i: (b, h, 0, 0)),
            pl.BlockSpec((1, Q_TILE, HEAD_PAD), lambda b, h, i: (b, i, h)),
            pl.BlockSpec((1, Q_TILE, 1), lambda b, h, i: (b, i, 0)),
            pl.BlockSpec((1, 1, S), lambda b, h, i: (b, 0, 0)),
            pl.BlockSpec((1, 1, LANES), lambda b, h, i: (h, 0, 0)),
            vec(DIFF_HD), vec(DIFF_HD), vec(DIFF_HD), vec(DIFF_HD),
            vec(2 * DIFF_HD),
        ],
        out_specs=pl.BlockSpec((1, Q_TILE, HEAD_PAD), lambda b, h, i: (b, i, h)),
        scratch_shapes=[
            pltpu.VMEM((2, Q_TILE, 1), F32),
            pltpu.VMEM((2, Q_TILE, 1), F32),
            pltpu.VMEM((2, Q_TILE, HEAD_PAD), F32),
        ],
        compiler_params=_cparams(("parallel", "parallel", "parallel")),
        name="diff_attn",
    )(q, kt, v, sg, pos_col, pos_row, slopes, lq1, lk1, lq2, lk2, hg)


def _post_kernel(o_ref, x_ref, mod_ref, wo_ref, fg_ref, out_ref, *, final):
    y = _dot(o_ref[0], wo_ref[...])
    gate = mod_ref[0, :, 2 * D_MODEL:]
    x_new = x_ref[0] + gate * y
    if final:
        x_new = _rms(x_new, fg_ref[...])
    out_ref[0] = x_new


def _post(o, x, mod_i, wo, fg, final):
    B, S, _ = x.shape
    T = ROW_TILE
    const = lambda shape: pl.BlockSpec(shape, lambda b, t: (0,) * len(shape))
    return pl.pallas_call(
        functools.partial(_post_kernel, final=final),
        out_shape=jax.ShapeDtypeStruct((B, S, D_MODEL), F32),
        grid=(B, S // T),
        in_specs=[
            pl.BlockSpec((1, T, D_MODEL), lambda b, t: (b, t, 0)),
            pl.BlockSpec((1, T, D_MODEL), lambda b, t: (b, t, 0)),
            pl.BlockSpec((1, 1, 3 * D_MODEL), lambda b, t: (b, 0, 0)),
            const(wo.shape),
            const((1, D_MODEL)),
        ],
        out_specs=pl.BlockSpec((1, T, D_MODEL), lambda b, t: (b, t, 0)),
        compiler_params=_cparams(("parallel", "parallel")),
        name="post_final" if final else "post",
    )(o, x, mod_i, wo, fg)


def _rot_cols(w):
    half = MLA_ROPE // 2
    return jnp.concatenate([-w[..., half:], w[..., :half]], axis=-1)


def _pad_slab(w, lo):
    pad = [(0, 0)] * (w.ndim - 1) + [(lo, HEAD_PAD - lo - w.shape[-1])]
    return jnp.pad(w, pad)


def _mla_weights(w_in, w_q_up, w_kv_up):
    o1 = MLA_Q_RANK
    o2 = o1 + MLA_KV_RANK
    o3 = o2 + MLA_ROPE
    w_kr = w_in[:, o2:o3]
    win = jnp.concatenate([
        w_in[:, :o2],
        _pad_slab(w_kr, MLA_NOPE),
        _pad_slab(_rot_cols(w_kr), MLA_NOPE),
        w_in[:, o3:],
    ], axis=1).astype(BF16)
    wq3 = w_q_up.reshape(MLA_Q_RANK, MLA_HEADS, MLA_QK)
    wq_plain = _pad_slab(wq3, 0).reshape(MLA_Q_RANK, MLA_HEADS * HEAD_PAD)
    wq_rot = _pad_slab(_rot_cols(wq3[..., MLA_NOPE:]), MLA_NOPE).reshape(MLA_Q_RANK, MLA_HEADS * HEAD_PAD)
    wq = jnp.concatenate([wq_plain, wq_rot], axis=1).astype(BF16)
    wkv3 = w_kv_up.reshape(MLA_KV_RANK, MLA_HEADS, MLA_NOPE + MLA_V)
    wk = _pad_slab(wkv3[..., :MLA_NOPE], 0).reshape(MLA_KV_RANK, MLA_HEADS * HEAD_PAD)
    wv = _pad_slab(wkv3[..., MLA_NOPE:], 0).reshape(MLA_KV_RANK, MLA_HEADS * HEAD_PAD)
    wkv = jnp.concatenate([wk, wv], axis=1).astype(BF16)
    return win, wq, wkv


def kernel(x, c, positions, ada_w, ada_b, norm_g, mla_w_in, mla_q_norm_g, mla_w_q_up, mla_kv_norm_g,
           mla_w_kv_up, mla_w_o, diff_w_in, diff_lq1, diff_lk1, diff_lq2, diff_lk2, diff_head_g,
           diff_w_o, final_g):
    B, S, D = x.shape
    assert D == D_MODEL and S % Q_TILE == 0 and S % K_TILE == 0 and S % ROW_TILE == 0
    c_pad = jnp.pad(c, ((0, 8 - B), (0, 0)))
    mod = _ada_mod(c_pad, ada_w, ada_b)[:, :B].reshape(DEPTH, B, 1, 3 * D)

    pos_col = positions.reshape(B, S, 1)
    pos_row = positions.reshape(B, 1, S)
    inv = ROPE_BASE ** (-jnp.arange(0, MLA_ROPE, 2, dtype=F32) / MLA_ROPE)
    invf = _pad_slab(jnp.concatenate([inv, inv]), MLA_NOPE).reshape(1, HEAD_PAD)
    slopes = jnp.exp2(-8.0 * jnp.arange(1, DIFF_HEADS + 1, dtype=F32) / DIFF_HEADS) * LOG2E
    slopes = jnp.broadcast_to(slopes[:, None, None], (DIFF_HEADS, 1, LANES))
    final_row = final_g.reshape(1, D)

    for i in range(DEPTH):
        j = i // 2
        g = norm_g[i].reshape(1, D)
        last = i == DEPTH - 1
        if i % 2 == 0:
            win, wq, wkv = _mla_weights(mla_w_in[j], mla_w_q_up[j], mla_w_kv_up[j])
            q, kt, v, sg = _mla_pre(x, mod[i], g, pos_col, invf, win,
                                    mla_q_norm_g[j].reshape(1, -1), wq,
                                    mla_kv_norm_g[j].reshape(1, -1), wkv)
            o = _mla_attn(q, kt, v, sg)
            wo = mla_w_o[j].astype(BF16)
        else:
            lambda_init = 0.8 - 0.6 * math.exp(-0.3 * i)
            q, kt, v, sg = _diff_pre(x, mod[i], g, diff_w_in[j].astype(BF16))
            o = _diff_attn(q, kt, v, sg, pos_col, pos_row, slopes,
                           diff_lq1[j].reshape(1, -1), diff_lk1[j].reshape(1, -1),
                           diff_lq2[j].reshape(1, -1), diff_lk2[j].reshape(1, -1),
                           diff_head_g[j].reshape(1, -1), lambda_init)
            wo = diff_w_o[j].astype(BF16)
        x = _post(o, x, mod[i], wo, final_row, last)
    return x
```

```python
import functools
import math

import jax
import jax.numpy as jnp
from jax import lax
from jax.experimental import pallas as pl
from jax.experimental.pallas import tpu as pltpu

D_MODEL = 1024
DEPTH = 4
EPS = 1e-6

MLA_HEADS = 16
MLA_NOPE = 64
MLA_ROPE = 32
MLA_V = 64
MLA_Q_RANK = 384
MLA_KV_RANK = 256
ROPE_BASE = 10000.0
MLA_QK = MLA_NOPE + MLA_ROPE

DIFF_HD = 64
DIFF_HEADS = D_MODEL // (2 * DIFF_HD)
DIFF_W = DIFF_HEADS * 2 * DIFF_HD
DIFF_V = 2 * DIFF_HD

LANES = 128
BF16_SUBLANES = 16
HEAD_PAD = LANES
LOG2E = math.log2(math.e)

MLA_VT_ROWS = MLA_V + BF16_SUBLANES
DIFF_VT_ROWS = DIFF_V + BF16_SUBLANES

ROW_TILE = 256
Q_TILE = 512
K_TILE = 512
VMEM_LIMIT = 56 * 1024 * 1024

BF16 = jnp.bfloat16
F32 = jnp.float32


def _cparams(sem):
    return pltpu.CompilerParams(dimension_semantics=sem, vmem_limit_bytes=VMEM_LIMIT)


def _dot(a, b):
    return jnp.dot(a, b, preferred_element_type=F32)


def _rms(x, g):
    return x * lax.rsqrt(jnp.mean(x * x, axis=-1, keepdims=True) + EPS) * g


def _silu(x):
    return x * (1.0 / (1.0 + jnp.exp(-x)))


def _mod_kernel(c_ref, w_ref, b_ref, o_ref):
    c = c_ref[...]
    o_ref[0] = _dot(_silu(c).astype(BF16), w_ref[0].astype(BF16)) + b_ref[0]


def _ada_mod(c_pad, ada_w, ada_b):
    rows = c_pad.shape[0]
    nblk = 3
    return pl.pallas_call(
        _mod_kernel,
        out_shape=jax.ShapeDtypeStruct((DEPTH, rows, 3 * D_MODEL), F32),
        grid=(DEPTH, nblk),
        in_specs=[
            pl.BlockSpec((rows, D_MODEL), lambda i, n: (0, 0)),
            pl.BlockSpec((1, D_MODEL, D_MODEL), lambda i, n: (i, 0, n)),
            pl.BlockSpec((1, 1, D_MODEL), lambda i, n: (i, 0, n)),
        ],
        out_specs=pl.BlockSpec((1, rows, D_MODEL), lambda i, n: (i, 0, n)),
        compiler_params=_cparams(("arbitrary", "arbitrary")),
        name="ada_mod",
    )(c_pad, ada_w, ada_b.reshape(DEPTH, 1, 3 * D_MODEL))


def _modulated_norm(x_ref, mod_ref, g_ref):
    x = x_ref[0]
    shift = mod_ref[0, :, 0:D_MODEL]
    scale = mod_ref[0, :, D_MODEL:2 * D_MODEL]
    return _rms(x, g_ref[...]) * (1.0 + scale) + shift


def _pipelined_key_loop(n_kv, scores_fn, values_fn, s_ref, m_ref, acc_ref):
    def produce(t, slot):
        col_max = []
        for c, s in enumerate(scores_fn(t)):
            s_ref[slot, c] = s
            col_max.append(jnp.max(s, axis=0, keepdims=True))
        return tuple(col_max)

    def consume(t, slot, col_max):
        for c, vt in enumerate(values_fn(t)):
            m_prev = m_ref[c]
            m_new = jnp.maximum(m_prev, col_max[c])
            p = jnp.exp2(s_ref[slot, c] - m_new).astype(BF16)
            alpha = jnp.exp2(m_prev - m_new)
            m_ref[c] = m_new
            acc_ref[c] = alpha * acc_ref[c] + _dot(vt, p)

    def pair(t0, col_max, last):
        cm1 = produce(t0 + 1, 1)
        consume(t0, 0, col_max)
        cm2 = None if last else produce(t0 + 2, 0)
        consume(t0 + 1, 1, cm1)
        return cm2

    assert n_kv % 2 == 0
    cm = produce(0, 0)
    cm = lax.fori_loop(0, n_kv // 2 - 1, lambda i, cm: pair(2 * i, cm, False), cm)
    pair(n_kv - 2, cm, True)


def _mla_pre_kernel(x_ref, mod_ref, g_ref, pos_ref, invf_ref, win_ref, qg_ref, wq_ref,
                    kvg_ref, wkv_ref, qt_out, k_out, vt_out, sg_out):
    h = _modulated_norm(x_ref, mod_ref, g_ref).astype(BF16)
    proj = _dot(h, win_ref[...])
    o1 = MLA_Q_RANK
    o2 = o1 + MLA_KV_RANK
    o3 = o2 + HEAD_PAD
    o4 = o3 + HEAD_PAD
    qn = _rms(proj[:, :o1], qg_ref[...]).astype(BF16)
    kvn = _rms(proj[:, o1:o2], kvg_ref[...]).astype(BF16)
    ang = pos_ref[0].astype(F32) * invf_ref[...]
    cos = jnp.cos(ang)
    sin = jnp.sin(ang)
    k_rope = proj[:, o2:o3] * cos + proj[:, o3:o4] * sin
    sg_out[0] = _silu(proj[:, o4:])

    qq = _dot(qn, wq_ref[...])
    kv = _dot(kvn, wkv_ref[...])
    half = MLA_HEADS * HEAD_PAD
    qscale = (MLA_QK ** -0.5) * LOG2E
    lane = lax.broadcasted_iota(jnp.int32, (1, HEAD_PAD), 1)
    ones_col = (lane == MLA_V).astype(F32)
    for hd in range(MLA_HEADS):
        a = hd * HEAD_PAD
        b = a + HEAD_PAD
        q_h = (qq[:, a:b] * cos + qq[:, half + a:half + b] * sin) * qscale
        qt_out[0, hd] = q_h.T.astype(BF16)
        k_out[0, hd] = (kv[:, a:b] + k_rope).astype(BF16)
        v_h = kv[:, half + a:half + b] + ones_col
        vt_out[0, hd] = v_h.T[:MLA_VT_ROWS].astype(BF16)


def _mla_pre(x, mod_i, g, pos_col, invf, win, qg, wq, kvg, wkv):
    B, S, _ = x.shape
    T = ROW_TILE
    const = lambda shape: pl.BlockSpec(shape, lambda b, t: (0,) * len(shape))
    return pl.pallas_call(
        _mla_pre_kernel,
        out_shape=(
            jax.ShapeDtypeStruct((B, MLA_HEADS, HEAD_PAD, S), BF16),
            jax.ShapeDtypeStruct((B, MLA_HEADS, S, HEAD_PAD), BF16),
            jax.ShapeDtypeStruct((B, MLA_HEADS, MLA_VT_ROWS, S), BF16),
            jax.ShapeDtypeStruct((B, S, D_MODEL), F32),
        ),
        grid=(B, S // T),
        in_specs=[
            pl.BlockSpec((1, T, D_MODEL), lambda b, t: (b, t, 0)),
            pl.BlockSpec((1, 1, 3 * D_MODEL), lambda b, t: (b, 0, 0)),
            const((1, D_MODEL)),
            pl.BlockSpec((1, T, 1), lambda b, t: (b, t, 0)),
            const((1, HEAD_PAD)),
            const(win.shape),
            const((1, MLA_Q_RANK)),
            const(wq.shape),
            const((1, MLA_KV_RANK)),
            const(wkv.shape),
        ],
        out_specs=(
            pl.BlockSpec((1, MLA_HEADS, HEAD_PAD, T), lambda b, t: (b, 0, 0, t)),
            pl.BlockSpec((1, MLA_HEADS, T, HEAD_PAD), lambda b, t: (b, 0, t, 0)),
            pl.BlockSpec((1, MLA_HEADS, MLA_VT_ROWS, T), lambda b, t: (b, 0, 0, t)),
            pl.BlockSpec((1, T, D_MODEL), lambda b, t: (b, t, 0)),
        ),
        compiler_params=_cparams(("parallel", "parallel")),
        name="mla_pre",
    )(x, mod_i, g, pos_col, invf, win, qg, wq, kvg, wkv)


MLA_HEADS_PER_STEP = 2


def _key_slice(t):
    return pl.ds(pl.multiple_of(t * K_TILE, K_TILE), K_TILE)


def _mla_attn_kernel(qt_ref, k_ref, vt_ref, sg_ref, o_ref, s_ref, m_ref, acc_ref):
    n_kv = k_ref.shape[2] // K_TILE
    m_ref[...] = jnp.full(m_ref.shape, -jnp.inf, F32)
    acc_ref[...] = jnp.zeros(acc_ref.shape, F32)

    heads = range(MLA_HEADS_PER_STEP)

    def scores_fn(t):
        return [_dot(k_ref[0, hh, _key_slice(t), :], qt_ref[0, hh]) for hh in heads]

    def values_fn(t):
        return [vt_ref[0, hh, :, _key_slice(t)] for hh in heads]

    _pipelined_key_loop(n_kv, scores_fn, values_fn, s_ref, m_ref, acc_ref)

    outs = []
    for hh in range(MLA_HEADS_PER_STEP):
        acc = acc_ref[hh]
        outs.append(acc[:MLA_V] / acc[MLA_V:MLA_V + 1])
    o = jnp.concatenate(outs, axis=0).T
    o_ref[0] = (o * sg_ref[0]).astype(BF16)


def _mla_attn(qt, k, vt, sg):
    B, H, S, _ = k.shape
    G = MLA_HEADS_PER_STEP
    return pl.pallas_call(
        _mla_attn_kernel,
        out_shape=jax.ShapeDtypeStruct((B, S, H * MLA_V), BF16),
        grid=(B, H // G, S // Q_TILE),
        in_specs=[
            pl.BlockSpec((1, G, HEAD_PAD, Q_TILE), lambda b, h, i: (b, h, 0, i)),
            pl.BlockSpec((1, G, S, HEAD_PAD), lambda b, h, i: (b, h, 0, 0)),
            pl.BlockSpec((1, G, MLA_VT_ROWS, S), lambda b, h, i: (b, h, 0, 0)),
            pl.BlockSpec((1, Q_TILE, G * MLA_V), lambda b, h, i: (b, i, h)),
        ],
        out_specs=pl.BlockSpec((1, Q_TILE, G * MLA_V), lambda b, h, i: (b, i, h)),
        scratch_shapes=[
            pltpu.VMEM((2, G, K_TILE, Q_TILE), F32),
            pltpu.VMEM((G, 1, Q_TILE), F32),
            pltpu.VMEM((G, MLA_VT_ROWS, Q_TILE), F32),
        ],
        compiler_params=_cparams(("parallel", "parallel", "parallel")),
        name="mla_attn",
    )(qt, k, vt, sg)


def _diff_pre_kernel(x_ref, mod_ref, g_ref, win_ref, qt_out, k_out, vt_out, sg_out):
    h = _modulated_norm(x_ref, mod_ref, g_ref).astype(BF16)
    proj = _dot(h, win_ref[...])
    qscale = (DIFF_HD ** -0.5) * LOG2E
    sg_out[0] = _silu(proj[:, 3 * DIFF_W:])
    T = proj.shape[0]
    sub = lax.broadcasted_iota(jnp.int32, (BF16_SUBLANES, T), 0)
    ones_rows = (sub == 0).astype(F32)
    for hd in range(DIFF_HEADS):
        a = hd * HEAD_PAD
        b = a + HEAD_PAD
        qt_out[0, hd] = (proj[:, a:b] * qscale).T.astype(BF16)
        k_out[0, hd] = proj[:, DIFF_W + a:DIFF_W + b].astype(BF16)
        v_t = proj[:, 2 * DIFF_W + a:2 * DIFF_W + b].T
        vt_out[0, hd] = jnp.concatenate([v_t, ones_rows], axis=0).astype(BF16)


def _diff_pre(x, mod_i, g, win):
    B, S, _ = x.shape
    T = ROW_TILE
    const = lambda shape: pl.BlockSpec(shape, lambda b, t: (0,) * len(shape))
    return pl.pallas_call(
        _diff_pre_kernel,
        out_shape=(
            jax.ShapeDtypeStruct((B, DIFF_HEADS, HEAD_PAD, S), BF16),
            jax.ShapeDtypeStruct((B, DIFF_HEADS, S, HEAD_PAD), BF16),
            jax.ShapeDtypeStruct((B, DIFF_HEADS, DIFF_VT_ROWS, S), BF16),
            jax.ShapeDtypeStruct((B, S, D_MODEL), F32),
        ),
        grid=(B, S // T),
        in_specs=[
            pl.BlockSpec((1, T, D_MODEL), lambda b, t: (b, t, 0)),
            pl.BlockSpec((1, 1, 3 * D_MODEL), lambda b, t: (b, 0, 0)),
            const((1, D_MODEL)),
            const(win.shape),
        ],
        out_specs=(
            pl.BlockSpec((1, DIFF_HEADS, HEAD_PAD, T), lambda b, t: (b, 0, 0, t)),
            pl.BlockSpec((1, DIFF_HEADS, T, HEAD_PAD), lambda b, t: (b, 0, t, 0)),
            pl.BlockSpec((1, DIFF_HEADS, DIFF_VT_ROWS, T), lambda b, t: (b, 0, 0, t)),
            pl.BlockSpec((1, T, D_MODEL), lambda b, t: (b, t, 0)),
        ),
        compiler_params=_cparams(("parallel", "parallel")),
        name="diff_pre",
    )(x, mod_i, g, win)


def _diff_attn_kernel(qt_ref, k_ref, vt_ref, sg_ref, posq_ref, posk_ref, slope_ref,
                      lq1_ref, lk1_ref, lq2_ref, lk2_ref, hg_ref, o_ref,
                      s_ref, m_ref, acc_ref, *, lambda_init):
    n_kv = k_ref.shape[2] // K_TILE
    m_ref[...] = jnp.full(m_ref.shape, -jnp.inf, F32)
    acc_ref[...] = jnp.zeros(acc_ref.shape, F32)

    qt = qt_ref[0, 0]
    row = lax.broadcasted_iota(jnp.int32, qt.shape, 0)
    zero = jnp.zeros_like(qt)
    qt_maps = (jnp.where(row < DIFF_HD, qt, zero), jnp.where(row >= DIFF_HD, qt, zero))
    pos_q = posq_ref[0].astype(F32)
    slope = slope_ref[0][:, 0:1]

    def scores_fn(t):
        pos_k = posk_ref[0, _key_slice(t), :].astype(F32)
        bias = jnp.abs(pos_k - pos_q) * slope
        kk = k_ref[0, 0, _key_slice(t), :]
        return [_dot(kk, qt_c) - bias for qt_c in qt_maps]

    def values_fn(t):
        vt = vt_ref[0, 0, :, _key_slice(t)]
        return [vt, vt]

    _pipelined_key_loop(n_kv, scores_fn, values_fn, s_ref, m_ref, acc_ref)

    lam = (jnp.exp(jnp.sum(lq1_ref[...] * lk1_ref[...], axis=-1, keepdims=True))
           - jnp.exp(jnp.sum(lq2_ref[...] * lk2_ref[...], axis=-1, keepdims=True))
           + lambda_init)
    a1 = acc_ref[0]
    a2 = acc_ref[1]
    o = a1[:DIFF_V] / a1[DIFF_V:DIFF_V + 1] - lam * (a2[:DIFF_V] / a2[DIFF_V:DIFF_V + 1])
    o = o * lax.rsqrt(jnp.mean(o * o, axis=0, keepdims=True) + EPS)
    o = o.T * (hg_ref[...] * (1.0 - lambda_init))
    o_ref[0] = (o * sg_ref[0]).astype(BF16)


def _diff_attn(qt, k, vt, sg, pos_col, pos_row, slopes, lq1, lk1, lq2, lk2, hg, lambda_init):
    B, H, S, _ = k.shape
    vec = lambda n: pl.BlockSpec((1, n), lambda b, h, i: (0, 0))
    return pl.pallas_call(
        functools.partial(_diff_attn_kernel, lambda_init=lambda_init),
        out_shape=jax.ShapeDtypeStruct((B, S, DIFF_W), BF16),
        grid=(B, H, S // Q_TILE),
        in_specs=[
            pl.BlockSpec((1, 1, HEAD_PAD, Q_TILE), lambda b, h, i: (b, h, 0, i)),
            pl.BlockSpec((1, 1, S, HEAD_PAD), lambda b, h, i: (b, h, 0, 0)),
            pl.BlockSpec((1, 1, DIFF_VT_ROWS, S), lambda b, h, i: (b, h, 0, 0)),
            pl.BlockSpec((1, Q_TILE, HEAD_PAD), lambda b, h, i: (b, i, h)),
            pl.BlockSpec((1, 1, Q_TILE), lambda b, h, i: (b, 0, i)),
            pl.BlockSpec((1, S, 1), lambda b, h, i: (b, 0, 0)),
            pl.BlockSpec((1, 1, LANES), lambda b, h, i: (h, 0, 0)),
            vec(DIFF_HD), vec(DIFF_HD), vec(DIFF_HD), vec(DIFF_HD),
            vec(DIFF_V),
        ],
        out_specs=pl.BlockSpec((1, Q_TILE, HEAD_PAD), lambda b, h, i: (b, i, h)),
        scratch_shapes=[
            pltpu.VMEM((2, 2, K_TILE, Q_TILE), F32),
            pltpu.VMEM((2, 1, Q_TILE), F32),
            pltpu.VMEM((2, DIFF_VT_ROWS, Q_TILE), F32),
        ],
        compiler_params=_cparams(("parallel", "parallel", "parallel")),
        name="diff_attn",
    )(qt, k, vt, sg, pos_row, pos_col, slopes, lq1, lk1, lq2, lk2, hg)


def _post_kernel(o_ref, x_ref, mod_ref, wo_ref, fg_ref, out_ref, *, final):
    y = _dot(o_ref[0], wo_ref[...])
    gate = mod_ref[0, :, 2 * D_MODEL:]
    x_new = x_ref[0] + gate * y
    if final:
        x_new = _rms(x_new, fg_ref[...])
    out_ref[0] = x_new


def _post(o, x, mod_i, wo, fg, final):
    B, S, _ = x.shape
    T = ROW_TILE
    const = lambda shape: pl.BlockSpec(shape, lambda b, t: (0,) * len(shape))
    return pl.pallas_call(
        functools.partial(_post_kernel, final=final),
        out_shape=jax.ShapeDtypeStruct((B, S, D_MODEL), F32),
        grid=(B, S // T),
        in_specs=[
            pl.BlockSpec((1, T, D_MODEL), lambda b, t: (b, t, 0)),
            pl.BlockSpec((1, T, D_MODEL), lambda b, t: (b, t, 0)),
            pl.BlockSpec((1, 1, 3 * D_MODEL), lambda b, t: (b, 0, 0)),
            const(wo.shape),
            const((1, D_MODEL)),
        ],
        out_specs=pl.BlockSpec((1, T, D_MODEL), lambda b, t: (b, t, 0)),
        compiler_params=_cparams(("parallel", "parallel")),
        name="post_final" if final else "post",
    )(o, x, mod_i, wo, fg)


def _rot_cols(w):
    half = MLA_ROPE // 2
    return jnp.concatenate([-w[..., half:], w[..., :half]], axis=-1)


def _pad_slab(w, lo):
    pad = [(0, 0)] * (w.ndim - 1) + [(lo, HEAD_PAD - lo - w.shape[-1])]
    return jnp.pad(w, pad)


def _mla_weights(w_in, w_q_up, w_kv_up):
    o1 = MLA_Q_RANK
    o2 = o1 + MLA_KV_RANK
    o3 = o2 + MLA_ROPE
    w_kr = w_in[:, o2:o3]
    win = jnp.concatenate([
        w_in[:, :o2],
        _pad_slab(w_kr, MLA_NOPE),
        _pad_slab(_rot_cols(w_kr), MLA_NOPE),
        w_in[:, o3:],
    ], axis=1).astype(BF16)
    wq3 = w_q_up.reshape(MLA_Q_RANK, MLA_HEADS, MLA_QK)
    wq_plain = _pad_slab(wq3, 0).reshape(MLA_Q_RANK, MLA_HEADS * HEAD_PAD)
    wq_rot = _pad_slab(_rot_cols(wq3[..., MLA_NOPE:]), MLA_NOPE).reshape(MLA_Q_RANK, MLA_HEADS * HEAD_PAD)
    wq = jnp.concatenate([wq_plain, wq_rot], axis=1).astype(BF16)
    wkv3 = w_kv_up.reshape(MLA_KV_RANK, MLA_HEADS, MLA_NOPE + MLA_V)
    wk = _pad_slab(wkv3[..., :MLA_NOPE], 0).reshape(MLA_KV_RANK, MLA_HEADS * HEAD_PAD)
    wv = _pad_slab(wkv3[..., MLA_NOPE:], 0).reshape(MLA_KV_RANK, MLA_HEADS * HEAD_PAD)
    wkv = jnp.concatenate([wk, wv], axis=1).astype(BF16)
    return win, wq, wkv


def kernel(x, c, positions, ada_w, ada_b, norm_g, mla_w_in, mla_q_norm_g, mla_w_q_up, mla_kv_norm_g,
           mla_w_kv_up, mla_w_o, diff_w_in, diff_lq1, diff_lk1, diff_lq2, diff_lk2, diff_head_g,
           diff_w_o, final_g):
    B, S, D = x.shape
    assert D == D_MODEL and S % Q_TILE == 0 and S % K_TILE == 0 and S % ROW_TILE == 0
    c_pad = jnp.pad(c, ((0, 8 - B), (0, 0)))
    mod = _ada_mod(c_pad, ada_w, ada_b)[:, :B].reshape(DEPTH, B, 1, 3 * D)

    pos_col = positions.reshape(B, S, 1)
    pos_row = positions.reshape(B, 1, S)
    inv = ROPE_BASE ** (-jnp.arange(0, MLA_ROPE, 2, dtype=F32) / MLA_ROPE)
    invf = _pad_slab(jnp.concatenate([inv, inv]), MLA_NOPE).reshape(1, HEAD_PAD)
    slopes = jnp.exp2(-8.0 * jnp.arange(1, DIFF_HEADS + 1, dtype=F32) / DIFF_HEADS) * LOG2E
    slopes = jnp.broadcast_to(slopes[:, None, None], (DIFF_HEADS, 1, LANES))
    final_row = final_g.reshape(1, D)

    for i in range(DEPTH):
        j = i // 2
        g = norm_g[i].reshape(1, D)
        last = i == DEPTH - 1
        if i % 2 == 0:
            win, wq, wkv = _mla_weights(mla_w_in[j], mla_w_q_up[j], mla_w_kv_up[j])
            qt, k, vt, sg = _mla_pre(x, mod[i], g, pos_col, invf, win,
                                     mla_q_norm_g[j].reshape(1, -1), wq,
                                     mla_kv_norm_g[j].reshape(1, -1), wkv)
            o = _mla_attn(qt, k, vt, sg)
            wo = mla_w_o[j].astype(BF16)
        else:
            lambda_init = 0.8 - 0.6 * math.exp(-0.3 * i)
            qt, k, vt, sg = _diff_pre(x, mod[i], g, diff_w_in[j].astype(BF16))
            o = _diff_attn(qt, k, vt, sg, pos_col, pos_row, slopes,
                           diff_lq1[j].reshape(1, -1), diff_lk1[j].reshape(1, -1),
                           diff_lq2[j].reshape(1, -1), diff_lk2[j].reshape(1, -1),
                           diff_head_g[j].reshape(1, -1), lambda_init)
            wo = diff_w_o[j].astype(BF16)
        x = _post(o, x, mod[i], wo, final_row, last)
    return x
```

```python
import functools
import math

import jax
import jax.numpy as jnp
from jax import lax
from jax.experimental import pallas as pl
from jax.experimental.pallas import tpu as pltpu

D_MODEL = 1024
DEPTH = 4
EPS = 1e-6

MLA_HEADS = 16
MLA_NOPE = 64
MLA_ROPE = 32
MLA_V = 64
MLA_Q_RANK = 384
MLA_KV_RANK = 256
ROPE_BASE = 10000.0
MLA_QK = MLA_NOPE + MLA_ROPE

DIFF_HD = 64
DIFF_HEADS = D_MODEL // (2 * DIFF_HD)
DIFF_W = DIFF_HEADS * 2 * DIFF_HD
DIFF_V = 2 * DIFF_HD

LANES = 128
BF16_SUBLANES = 16
HEAD_PAD = LANES
LOG2E = math.log2(math.e)

MLA_VT_ROWS = MLA_V + BF16_SUBLANES
DIFF_VT_ROWS = DIFF_V + BF16_SUBLANES

ROW_TILE = 256
Q_TILE = 1024
K_TILE = 512
VMEM_LIMIT = 56 * 1024 * 1024

BF16 = jnp.bfloat16
F32 = jnp.float32


def _cparams(sem):
    return pltpu.CompilerParams(dimension_semantics=sem, vmem_limit_bytes=VMEM_LIMIT)


def _dot(a, b):
    return jnp.dot(a, b, preferred_element_type=F32)


def _rms(x, g):
    return x * lax.rsqrt(jnp.mean(x * x, axis=-1, keepdims=True) + EPS) * g


def _silu(x):
    return x * (1.0 / (1.0 + jnp.exp(-x)))


def _mod_kernel(c_ref, w_ref, b_ref, o_ref):
    c = c_ref[...]
    o_ref[0] = _dot(_silu(c).astype(BF16), w_ref[0].astype(BF16)) + b_ref[0]


def _ada_mod(c_pad, ada_w, ada_b):
    rows = c_pad.shape[0]
    nblk = 3
    return pl.pallas_call(
        _mod_kernel,
        out_shape=jax.ShapeDtypeStruct((DEPTH, rows, 3 * D_MODEL), F32),
        grid=(DEPTH, nblk),
        in_specs=[
            pl.BlockSpec((rows, D_MODEL), lambda i, n: (0, 0)),
            pl.BlockSpec((1, D_MODEL, D_MODEL), lambda i, n: (i, 0, n)),
            pl.BlockSpec((1, 1, D_MODEL), lambda i, n: (i, 0, n)),
        ],
        out_specs=pl.BlockSpec((1, rows, D_MODEL), lambda i, n: (i, 0, n)),
        compiler_params=_cparams(("arbitrary", "arbitrary")),
        name="ada_mod",
    )(c_pad, ada_w, ada_b.reshape(DEPTH, 1, 3 * D_MODEL))


def _modulated_norm(x_ref, mod_ref, g_ref):
    x = x_ref[0]
    shift = mod_ref[0, :, 0:D_MODEL]
    scale = mod_ref[0, :, D_MODEL:2 * D_MODEL]
    return _rms(x, g_ref[...]) * (1.0 + scale) + shift


def _pipelined_key_loop(n_kv, scores_fn, values_fn, s_ref, m_ref, acc_ref):
    def step(t_cons, slot, col_max, t_prod):
        thunks = scores_fn(t_prod) if t_prod is not None else None
        next_max = []
        for c, vt in enumerate(values_fn(t_cons)):
            if thunks is not None:
                s = thunks[c]()
                s_ref[1 - slot, c] = s
                next_max.append(jnp.max(s, axis=0, keepdims=True))
            m_prev = m_ref[c]
            m_new = jnp.maximum(m_prev, col_max[c])
            p = jnp.exp2(s_ref[slot, c] - m_new).astype(BF16)
            alpha = jnp.exp2(m_prev - m_new)
            m_ref[c] = m_new
            acc_ref[c] = alpha * acc_ref[c] + _dot(vt, p)
        return tuple(next_max)

    def pair(t0, col_max, last):
        cm1 = step(t0, 0, col_max, t0 + 1)
        return step(t0 + 1, 1, cm1, None if last else t0 + 2)

    assert n_kv % 2 == 0
    cm = []
    for c, thunk in enumerate(scores_fn(0)):
        s = thunk()
        s_ref[0, c] = s
        cm.append(jnp.max(s, axis=0, keepdims=True))
    cm = lax.fori_loop(0, n_kv // 2 - 1, lambda i, cm: pair(2 * i, cm, False), tuple(cm))
    pair(n_kv - 2, cm, True)


def _mla_pre_kernel(x_ref, mod_ref, g_ref, pos_ref, invf_ref, win_ref, qg_ref, wq_ref,
                    kvg_ref, wkv_ref, qt_out, k_out, vt_out, sg_out):
    h = _modulated_norm(x_ref, mod_ref, g_ref).astype(BF16)
    proj = _dot(h, win_ref[...])
    o1 = MLA_Q_RANK
    o2 = o1 + MLA_KV_RANK
    o3 = o2 + HEAD_PAD
    o4 = o3 + HEAD_PAD
    qn = _rms(proj[:, :o1], qg_ref[...]).astype(BF16)
    kvn = _rms(proj[:, o1:o2], kvg_ref[...]).astype(BF16)
    ang = pos_ref[0].astype(F32) * invf_ref[...]
    cos = jnp.cos(ang)
    sin = jnp.sin(ang)
    k_rope = proj[:, o2:o3] * cos + proj[:, o3:o4] * sin
    sg_out[0] = _silu(proj[:, o4:])

    qq = _dot(qn, wq_ref[...])
    kv = _dot(kvn, wkv_ref[...])
    half = MLA_HEADS * HEAD_PAD
    qscale = (MLA_QK ** -0.5) * LOG2E
    lane = lax.broadcasted_iota(jnp.int32, (1, HEAD_PAD), 1)
    ones_col = (lane == MLA_V).astype(F32)
    for hd in range(MLA_HEADS):
        a = hd * HEAD_PAD
        b = a + HEAD_PAD
        q_h = (qq[:, a:b] * cos + qq[:, half + a:half + b] * sin) * qscale
        qt_out[0, hd] = q_h.T.astype(BF16)
        k_out[0, hd] = (kv[:, a:b] + k_rope).astype(BF16)
        v_h = kv[:, half + a:half + b] + ones_col
        vt_out[0, hd] = v_h.T[:MLA_VT_ROWS].astype(BF16)


def _mla_pre(x, mod_i, g, pos_col, invf, win, qg, wq, kvg, wkv):
    B, S, _ = x.shape
    T = ROW_TILE
    const = lambda shape: pl.BlockSpec(shape, lambda b, t: (0,) * len(shape))
    return pl.pallas_call(
        _mla_pre_kernel,
        out_shape=(
            jax.ShapeDtypeStruct((B, MLA_HEADS, HEAD_PAD, S), BF16),
            jax.ShapeDtypeStruct((B, MLA_HEADS, S, HEAD_PAD), BF16),
            jax.ShapeDtypeStruct((B, MLA_HEADS, MLA_VT_ROWS, S), BF16),
            jax.ShapeDtypeStruct((B, S, D_MODEL), F32),
        ),
        grid=(B, S // T),
        in_specs=[
            pl.BlockSpec((1, T, D_MODEL), lambda b, t: (b, t, 0)),
            pl.BlockSpec((1, 1, 3 * D_MODEL), lambda b, t: (b, 0, 0)),
            const((1, D_MODEL)),
            pl.BlockSpec((1, T, 1), lambda b, t: (b, t, 0)),
            const((1, HEAD_PAD)),
            const(win.shape),
            const((1, MLA_Q_RANK)),
            const(wq.shape),
            const((1, MLA_KV_RANK)),
            const(wkv.shape),
        ],
        out_specs=(
            pl.BlockSpec((1, MLA_HEADS, HEAD_PAD, T), lambda b, t: (b, 0, 0, t)),
            pl.BlockSpec((1, MLA_HEADS, T, HEAD_PAD), lambda b, t: (b, 0, t, 0)),
            pl.BlockSpec((1, MLA_HEADS, MLA_VT_ROWS, T), lambda b, t: (b, 0, 0, t)),
            pl.BlockSpec((1, T, D_MODEL), lambda b, t: (b, t, 0)),
        ),
        compiler_params=_cparams(("parallel", "parallel")),
        name="mla_pre",
    )(x, mod_i, g, pos_col, invf, win, qg, wq, kvg, wkv)


MLA_HEADS_PER_STEP = 2


def _key_slice(t):
    return pl.ds(pl.multiple_of(t * K_TILE, K_TILE), K_TILE)


def _key_scores(k_ref, qt_ref, hh, t):
    return _dot(k_ref[0, hh, _key_slice(t), :], qt_ref[0, hh])


def _mla_attn_kernel(qt_ref, k_ref, vt_ref, sg_ref, o_ref, s_ref, m_ref, acc_ref):
    n_kv = k_ref.shape[2] // K_TILE
    m_ref[...] = jnp.full(m_ref.shape, -jnp.inf, F32)
    acc_ref[...] = jnp.zeros(acc_ref.shape, F32)

    heads = range(MLA_HEADS_PER_STEP)

    def scores_fn(t):
        return [functools.partial(_key_scores, k_ref, qt_ref, hh, t) for hh in heads]

    def values_fn(t):
        return [vt_ref[0, hh, :, _key_slice(t)] for hh in heads]

    _pipelined_key_loop(n_kv, scores_fn, values_fn, s_ref, m_ref, acc_ref)

    outs = []
    for hh in range(MLA_HEADS_PER_STEP):
        acc = acc_ref[hh]
        outs.append(acc[:MLA_V] / acc[MLA_V:MLA_V + 1])
    o = jnp.concatenate(outs, axis=0).T
    o_ref[0] = (o * sg_ref[0]).astype(BF16)


def _mla_attn(qt, k, vt, sg):
    B, H, S, _ = k.shape
    G = MLA_HEADS_PER_STEP
    return pl.pallas_call(
        _mla_attn_kernel,
        out_shape=jax.ShapeDtypeStruct((B, S, H * MLA_V), BF16),
        grid=(B, H // G, S // Q_TILE),
        in_specs=[
            pl.BlockSpec((1, G, HEAD_PAD, Q_TILE), lambda b, h, i: (b, h, 0, i)),
            pl.BlockSpec((1, G, S, HEAD_PAD), lambda b, h, i: (b, h, 0, 0)),
            pl.BlockSpec((1, G, MLA_VT_ROWS, S), lambda b, h, i: (b, h, 0, 0)),
            pl.BlockSpec((1, Q_TILE, G * MLA_V), lambda b, h, i: (b, i, h)),
        ],
        out_specs=pl.BlockSpec((1, Q_TILE, G * MLA_V), lambda b, h, i: (b, i, h)),
        scratch_shapes=[
            pltpu.VMEM((2, G, K_TILE, Q_TILE), F32),
            pltpu.VMEM((G, 1, Q_TILE), F32),
            pltpu.VMEM((G, MLA_VT_ROWS, Q_TILE), F32),
        ],
        compiler_params=_cparams(("parallel", "parallel", "parallel")),
        name="mla_attn",
    )(qt, k, vt, sg)


def _diff_pre_kernel(x_ref, mod_ref, g_ref, win_ref, qt_out, k_out, vt_out, sg_out):
    h = _modulated_norm(x_ref, mod_ref, g_ref).astype(BF16)
    proj = _dot(h, win_ref[...])
    qscale = (DIFF_HD ** -0.5) * LOG2E
    sg_out[0] = _silu(proj[:, 3 * DIFF_W:])
    T = proj.shape[0]
    sub = lax.broadcasted_iota(jnp.int32, (BF16_SUBLANES, T), 0)
    ones_rows = (sub == 0).astype(F32)
    for hd in range(DIFF_HEADS):
        a = hd * HEAD_PAD
        b = a + HEAD_PAD
        qt_out[0, hd] = (proj[:, a:b] * qscale).T.astype(BF16)
        k_out[0, hd] = proj[:, DIFF_W + a:DIFF_W + b].astype(BF16)
        v_t = proj[:, 2 * DIFF_W + a:2 * DIFF_W + b].T
        vt_out[0, hd] = jnp.concatenate([v_t, ones_rows], axis=0).astype(BF16)


def _diff_pre(x, mod_i, g, win):
    B, S, _ = x.shape
    T = ROW_TILE
    const = lambda shape: pl.BlockSpec(shape, lambda b, t: (0,) * len(shape))
    return pl.pallas_call(
        _diff_pre_kernel,
        out_shape=(
            jax.ShapeDtypeStruct((B, DIFF_HEADS, HEAD_PAD, S), BF16),
            jax.ShapeDtypeStruct((B, DIFF_HEADS, S, HEAD_PAD), BF16),
            jax.ShapeDtypeStruct((B, DIFF_HEADS, DIFF_VT_ROWS, S), BF16),
            jax.ShapeDtypeStruct((B, S, D_MODEL), F32),
        ),
        grid=(B, S // T),
        in_specs=[
            pl.BlockSpec((1, T, D_MODEL), lambda b, t: (b, t, 0)),
            pl.BlockSpec((1, 1, 3 * D_MODEL), lambda b, t: (b, 0, 0)),
            const((1, D_MODEL)),
            const(win.shape),
        ],
        out_specs=(
            pl.BlockSpec((1, DIFF_HEADS, HEAD_PAD, T), lambda b, t: (b, 0, 0, t)),
            pl.BlockSpec((1, DIFF_HEADS, T, HEAD_PAD), lambda b, t: (b, 0, t, 0)),
            pl.BlockSpec((1, DIFF_HEADS, DIFF_VT_ROWS, T), lambda b, t: (b, 0, 0, t)),
            pl.BlockSpec((1, T, D_MODEL), lambda b, t: (b, t, 0)),
        ),
        compiler_params=_cparams(("parallel", "parallel")),
        name="diff_pre",
    )(x, mod_i, g, win)


def _diff_attn_kernel(qt_ref, k_ref, vt_ref, sg_ref, posq_ref, posk_ref, slope_ref,
                      lq1_ref, lk1_ref, lq2_ref, lk2_ref, hg_ref, o_ref,
                      s_ref, m_ref, acc_ref, *, lambda_init):
    n_kv = k_ref.shape[2] // K_TILE
    m_ref[...] = jnp.full(m_ref.shape, -jnp.inf, F32)
    acc_ref[...] = jnp.zeros(acc_ref.shape, F32)

    qt = qt_ref[0, 0]
    row = lax.broadcasted_iota(jnp.int32, qt.shape, 0)
    zero = jnp.zeros_like(qt)
    qt_maps = (jnp.where(row < DIFF_HD, qt, zero), jnp.where(row >= DIFF_HD, qt, zero))
    pos_q = posq_ref[0].astype(F32)
    slope = slope_ref[0][:, 0:1]

    def scores_fn(t):
        pos_k = posk_ref[0, _key_slice(t), :].astype(F32)
        bias = jnp.abs(pos_k - pos_q) * slope
        kk = k_ref[0, 0, _key_slice(t), :]
        return [lambda qt_c=qt_c: _dot(kk, qt_c) - bias for qt_c in qt_maps]

    def values_fn(t):
        vt = vt_ref[0, 0, :, _key_slice(t)]
        return [vt, vt]

    _pipelined_key_loop(n_kv, scores_fn, values_fn, s_ref, m_ref, acc_ref)

    lam = (jnp.exp(jnp.sum(lq1_ref[...] * lk1_ref[...], axis=-1, keepdims=True))
           - jnp.exp(jnp.sum(lq2_ref[...] * lk2_ref[...], axis=-1, keepdims=True))
           + lambda_init)
    a1 = acc_ref[0]
    a2 = acc_ref[1]
    o = a1[:DIFF_V] / a1[DIFF_V:DIFF_V + 1] - lam * (a2[:DIFF_V] / a2[DIFF_V:DIFF_V + 1])
    o = o * lax.rsqrt(jnp.mean(o * o, axis=0, keepdims=True) + EPS)
    o = o.T * (hg_ref[...] * (1.0 - lambda_init))
    o_ref[0] = (o * sg_ref[0]).astype(BF16)


def _diff_attn(qt, k, vt, sg, pos_col, pos_row, slopes, lq1, lk1, lq2, lk2, hg, lambda_init):
    B, H, S, _ = k.shape
    vec = lambda n: pl.BlockSpec((1, n), lambda b, h, i: (0, 0))
    return pl.pallas_call(
        functools.partial(_diff_attn_kernel, lambda_init=lambda_init),
        out_shape=jax.ShapeDtypeStruct((B, S, DIFF_W), BF16),
        grid=(B, H, S // Q_TILE),
        in_specs=[
            pl.BlockSpec((1, 1, HEAD_PAD, Q_TILE), lambda b, h, i: (b, h, 0, i)),
            pl.BlockSpec((1, 1, S, HEAD_PAD), lambda b, h, i: (b, h, 0, 0)),
            pl.BlockSpec((1, 1, DIFF_VT_ROWS, S), lambda b, h, i: (b, h, 0, 0)),
            pl.BlockSpec((1, Q_TILE, HEAD_PAD), lambda b, h, i: (b, i, h)),
            pl.BlockSpec((1, 1, Q_TILE), lambda b, h, i: (b, 0, i)),
            pl.BlockSpec((1, S, 1), lambda b, h, i: (b, 0, 0)),
            pl.BlockSpec((1, 1, LANES), lambda b, h, i: (h, 0, 0)),
            vec(DIFF_HD), vec(DIFF_HD), vec(DIFF_HD), vec(DIFF_HD),
            vec(DIFF_V),
        ],
        out_specs=pl.BlockSpec((1, Q_TILE, HEAD_PAD), lambda b, h, i: (b, i, h)),
        scratch_shapes=[
            pltpu.VMEM((2, 2, K_TILE, Q_TILE), F32),
            pltpu.VMEM((2, 1, Q_TILE), F32),
            pltpu.VMEM((2, DIFF_VT_ROWS, Q_TILE), F32),
        ],
        compiler_params=_cparams(("parallel", "parallel", "parallel")),
        name="diff_attn",
    )(qt, k, vt, sg, pos_row, pos_col, slopes, lq1, lk1, lq2, lk2, hg)


def _post_kernel(o_ref, x_ref, mod_ref, wo_ref, fg_ref, out_ref, *, final):
    y = _dot(o_ref[0], wo_ref[...])
    gate = mod_ref[0, :, 2 * D_MODEL:]
    x_new = x_ref[0] + gate * y
    if final:
        x_new = _rms(x_new, fg_ref[...])
    out_ref[0] = x_new


def _post(o, x, mod_i, wo, fg, final):
    B, S, _ = x.shape
    T = ROW_TILE
    const = lambda shape: pl.BlockSpec(shape, lambda b, t: (0,) * len(shape))
    return pl.pallas_call(
        functools.partial(_post_kernel, final=final),
        out_shape=jax.ShapeDtypeStruct((B, S, D_MODEL), F32),
        grid=(B, S // T),
        in_specs=[
            pl.BlockSpec((1, T, D_MODEL), lambda b, t: (b, t, 0)),
            pl.BlockSpec((1, T, D_MODEL), lambda b, t: (b, t, 0)),
            pl.BlockSpec((1, 1, 3 * D_MODEL), lambda b, t: (b, 0, 0)),
            const(wo.shape),
            const((1, D_MODEL)),
        ],
        out_specs=pl.BlockSpec((1, T, D_MODEL), lambda b, t: (b, t, 0)),
        compiler_params=_cparams(("parallel", "parallel")),
        name="post_final" if final else "post",
    )(o, x, mod_i, wo, fg)


def _rot_cols(w):
    half = MLA_ROPE // 2
    return jnp.concatenate([-w[..., half:], w[..., :half]], axis=-1)


def _pad_slab(w, lo):
    pad = [(0, 0)] * (w.ndim - 1) + [(lo, HEAD_PAD - lo - w.shape[-1])]
    return jnp.pad(w, pad)


def _mla_weights(w_in, w_q_up, w_kv_up):
    o1 = MLA_Q_RANK
    o2 = o1 + MLA_KV_RANK
    o3 = o2 + MLA_ROPE
    w_kr = w_in[:, o2:o3]
    win = jnp.concatenate([
        w_in[:, :o2],
        _pad_slab(w_kr, MLA_NOPE),
        _pad_slab(_rot_cols(w_kr), MLA_NOPE),
        w_in[:, o3:],
    ], axis=1).astype(BF16)
    wq3 = w_q_up.reshape(MLA_Q_RANK, MLA_HEADS, MLA_QK)
    wq_plain = _pad_slab(wq3, 0).reshape(MLA_Q_RANK, MLA_HEADS * HEAD_PAD)
    wq_rot = _pad_slab(_rot_cols(wq3[..., MLA_NOPE:]), MLA_NOPE).reshape(MLA_Q_RANK, MLA_HEADS * HEAD_PAD)
    wq = jnp.concatenate([wq_plain, wq_rot], axis=1).astype(BF16)
    wkv3 = w_kv_up.reshape(MLA_KV_RANK, MLA_HEADS, MLA_NOPE + MLA_V)
    wk = _pad_slab(wkv3[..., :MLA_NOPE], 0).reshape(MLA_KV_RANK, MLA_HEADS * HEAD_PAD)
    wv = _pad_slab(wkv3[..., MLA_NOPE:], 0).reshape(MLA_KV_RANK, MLA_HEADS * HEAD_PAD)
    wkv = jnp.concatenate([wk, wv], axis=1).astype(BF16)
    return win, wq, wkv


def kernel(x, c, positions, ada_w, ada_b, norm_g, mla_w_in, mla_q_norm_g, mla_w_q_up, mla_kv_norm_g,
           mla_w_kv_up, mla_w_o, diff_w_in, diff_lq1, diff_lk1, diff_lq2, diff_lk2, diff_head_g,
           diff_w_o, final_g):
    B, S, D = x.shape
    assert D == D_MODEL and S % Q_TILE == 0 and S % K_TILE == 0 and S % ROW_TILE == 0
    c_pad = jnp.pad(c, ((0, 8 - B), (0, 0)))
    mod = _ada_mod(c_pad, ada_w, ada_b)[:, :B].reshape(DEPTH, B, 1, 3 * D)

    pos_col = positions.reshape(B, S, 1)
    pos_row = positions.reshape(B, 1, S)
    inv = ROPE_BASE ** (-jnp.arange(0, MLA_ROPE, 2, dtype=F32) / MLA_ROPE)
    invf = _pad_slab(jnp.concatenate([inv, inv]), MLA_NOPE).reshape(1, HEAD_PAD)
    slopes = jnp.exp2(-8.0 * jnp.arange(1, DIFF_HEADS + 1, dtype=F32) / DIFF_HEADS) * LOG2E
    slopes = jnp.broadcast_to(slopes[:, None, None], (DIFF_HEADS, 1, LANES))
    final_row = final_g.reshape(1, D)

    for i in range(DEPTH):
        j = i // 2
        g = norm_g[i].reshape(1, D)
        last = i == DEPTH - 1
        if i % 2 == 0:
            win, wq, wkv = _mla_weights(mla_w_in[j], mla_w_q_up[j], mla_w_kv_up[j])
            qt, k, vt, sg = _mla_pre(x, mod[i], g, pos_col, invf, win,
                                     mla_q_norm_g[j].reshape(1, -1), wq,
                                     mla_kv_norm_g[j].reshape(1, -1), wkv)
            o = _mla_attn(qt, k, vt, sg)
            wo = mla_w_o[j].astype(BF16)
        else:
            lambda_init = 0.8 - 0.6 * math.exp(-0.3 * i)
            qt, k, vt, sg = _diff_pre(x, mod[i], g, diff_w_in[j].astype(BF16))
            o = _diff_attn(qt, k, vt, sg, pos_col, pos_row, slopes,
                           diff_lq1[j].reshape(1, -1), diff_lk1[j].reshape(1, -1),
                           diff_lq2[j].reshape(1, -1), diff_lk2[j].reshape(1, -1),
                           diff_head_g[j].reshape(1, -1), lambda_init)
            wo = diff_w_o[j].astype(BF16)
        x = _post(o, x, mod[i], wo, final_row, last)
    return x
```

```python
import functools
import math

import jax
import jax.numpy as jnp
from jax import lax
from jax.experimental import pallas as pl
from jax.experimental.pallas import tpu as pltpu

D_MODEL = 1024
DEPTH = 4
EPS = 1e-6

MLA_HEADS = 16
MLA_NOPE = 64
MLA_ROPE = 32
MLA_V = 64
MLA_Q_RANK = 384
MLA_KV_RANK = 256
ROPE_BASE = 10000.0
MLA_QK = MLA_NOPE + MLA_ROPE

DIFF_HD = 64
DIFF_HEADS = D_MODEL // (2 * DIFF_HD)
DIFF_W = DIFF_HEADS * 2 * DIFF_HD
DIFF_V = 2 * DIFF_HD

LANES = 128
BF16_SUBLANES = 16
HEAD_PAD = LANES
LOG2E = math.log2(math.e)

MLA_VT_ROWS = MLA_V + BF16_SUBLANES
DIFF_VT_ROWS = DIFF_V + BF16_SUBLANES

ROW_TILE = 256
Q_TILE = 1024
K_TILE = 512
KEY_TILES_PER_TRIP = 4
VMEM_LIMIT = 56 * 1024 * 1024

BF16 = jnp.bfloat16
F32 = jnp.float32


def _cparams(sem):
    return pltpu.CompilerParams(dimension_semantics=sem, vmem_limit_bytes=VMEM_LIMIT)


def _dot(a, b):
    return jnp.dot(a, b, preferred_element_type=F32)


def _rms(x, g):
    return x * lax.rsqrt(jnp.mean(x * x, axis=-1, keepdims=True) + EPS) * g


def _silu(x):
    return x * (1.0 / (1.0 + jnp.exp(-x)))


def _mod_kernel(c_ref, w_ref, b_ref, o_ref):
    c = c_ref[...]
    o_ref[0] = _dot(_silu(c).astype(BF16), w_ref[0].astype(BF16)) + b_ref[0]


def _ada_mod(c_pad, ada_w, ada_b):
    rows = c_pad.shape[0]
    nblk = 3
    return pl.pallas_call(
        _mod_kernel,
        out_shape=jax.ShapeDtypeStruct((DEPTH, rows, 3 * D_MODEL), F32),
        grid=(DEPTH, nblk),
        in_specs=[
            pl.BlockSpec((rows, D_MODEL), lambda i, n: (0, 0)),
            pl.BlockSpec((1, D_MODEL, D_MODEL), lambda i, n: (i, 0, n)),
            pl.BlockSpec((1, 1, D_MODEL), lambda i, n: (i, 0, n)),
        ],
        out_specs=pl.BlockSpec((1, rows, D_MODEL), lambda i, n: (i, 0, n)),
        compiler_params=_cparams(("arbitrary", "arbitrary")),
        name="ada_mod",
    )(c_pad, ada_w, ada_b.reshape(DEPTH, 1, 3 * D_MODEL))


def _modulated_norm(x_ref, mod_ref, g_ref):
    x = x_ref[0]
    shift = mod_ref[0, :, 0:D_MODEL]
    scale = mod_ref[0, :, D_MODEL:2 * D_MODEL]
    return _rms(x, g_ref[...]) * (1.0 + scale) + shift


def _pipelined_key_loop(n_kv, scores_fn, values_fn, s_ref, m_ref, acc_ref):
    def step(t_cons, slot, col_max, t_prod):
        thunks = scores_fn(t_prod) if t_prod is not None else None
        next_max = []
        for c, vt in enumerate(values_fn(t_cons)):
            if thunks is not None:
                s = thunks[c]()
                s_ref[1 - slot, c] = s
                next_max.append(jnp.max(s, axis=0, keepdims=True))
            m_prev = m_ref[c]
            m_new = jnp.maximum(m_prev, col_max[c])
            p = jnp.exp2(s_ref[slot, c] - m_new).astype(BF16)
            alpha = jnp.exp2(m_prev - m_new)
            m_ref[c] = m_new
            acc_ref[c] = alpha * acc_ref[c] + _dot(vt, p)
        return tuple(next_max)

    def group(t0, col_max, last):
        for u in range(KEY_TILES_PER_TRIP):
            final_step = last and u == KEY_TILES_PER_TRIP - 1
            col_max = step(t0 + u, u % 2, col_max, None if final_step else t0 + u + 1)
        return col_max

    assert KEY_TILES_PER_TRIP % 2 == 0 and n_kv % KEY_TILES_PER_TRIP == 0
    cm = []
    for c, thunk in enumerate(scores_fn(0)):
        s = thunk()
        s_ref[0, c] = s
        cm.append(jnp.max(s, axis=0, keepdims=True))
    trips = n_kv // KEY_TILES_PER_TRIP
    cm = lax.fori_loop(0, trips - 1, lambda i, cm: group(KEY_TILES_PER_TRIP * i, cm, False), tuple(cm))
    group(n_kv - KEY_TILES_PER_TRIP, cm, True)


def _mla_pre_kernel(x_ref, mod_ref, g_ref, pos_ref, invf_ref, win_ref, qg_ref, wq_ref,
                    kvg_ref, wkv_ref, qt_out, k_out, vt_out, sg_out):
    h = _modulated_norm(x_ref, mod_ref, g_ref).astype(BF16)
    proj = _dot(h, win_ref[...])
    o1 = MLA_Q_RANK
    o2 = o1 + MLA_KV_RANK
    o3 = o2 + HEAD_PAD
    o4 = o3 + HEAD_PAD
    qn = _rms(proj[:, :o1], qg_ref[...]).astype(BF16)
    kvn = _rms(proj[:, o1:o2], kvg_ref[...]).astype(BF16)
    ang = pos_ref[0].astype(F32) * invf_ref[...]
    cos = jnp.cos(ang)
    sin = jnp.sin(ang)
    k_rope = proj[:, o2:o3] * cos + proj[:, o3:o4] * sin
    sg_out[0] = _silu(proj[:, o4:])

    qq = _dot(qn, wq_ref[...])
    kv = _dot(kvn, wkv_ref[...])
    half = MLA_HEADS * HEAD_PAD
    qscale = (MLA_QK ** -0.5) * LOG2E
    lane = lax.broadcasted_iota(jnp.int32, (1, HEAD_PAD), 1)
    ones_col = (lane == MLA_V).astype(F32)
    for hd in range(MLA_HEADS):
        a = hd * HEAD_PAD
        b = a + HEAD_PAD
        q_h = (qq[:, a:b] * cos + qq[:, half + a:half + b] * sin) * qscale
        qt_out[0, hd] = q_h.T.astype(BF16)
        k_out[0, hd] = (kv[:, a:b] + k_rope).astype(BF16)
        v_h = kv[:, half + a:half + b] + ones_col
        vt_out[0, hd] = v_h.T[:MLA_VT_ROWS].astype(BF16)


def _mla_pre(x, mod_i, g, pos_col, invf, win, qg, wq, kvg, wkv):
    B, S, _ = x.shape
    T = ROW_TILE
    const = lambda shape: pl.BlockSpec(shape, lambda b, t: (0,) * len(shape))
    return pl.pallas_call(
        _mla_pre_kernel,
        out_shape=(
            jax.ShapeDtypeStruct((B, MLA_HEADS, HEAD_PAD, S), BF16),
            jax.ShapeDtypeStruct((B, MLA_HEADS, S, HEAD_PAD), BF16),
            jax.ShapeDtypeStruct((B, MLA_HEADS, MLA_VT_ROWS, S), BF16),
            jax.ShapeDtypeStruct((B, S, D_MODEL), F32),
        ),
        grid=(B, S // T),
        in_specs=[
            pl.BlockSpec((1, T, D_MODEL), lambda b, t: (b, t, 0)),
            pl.BlockSpec((1, 1, 3 * D_MODEL), lambda b, t: (b, 0, 0)),
            const((1, D_MODEL)),
            pl.BlockSpec((1, T, 1), lambda b, t: (b, t, 0)),
            const((1, HEAD_PAD)),
            const(win.shape),
            const((1, MLA_Q_RANK)),
            const(wq.shape),
            const((1, MLA_KV_RANK)),
            const(wkv.shape),
        ],
        out_specs=(
            pl.BlockSpec((1, MLA_HEADS, HEAD_PAD, T), lambda b, t: (b, 0, 0, t)),
            pl.BlockSpec((1, MLA_HEADS, T, HEAD_PAD), lambda b, t: (b, 0, t, 0)),
            pl.BlockSpec((1, MLA_HEADS, MLA_VT_ROWS, T), lambda b, t: (b, 0, 0, t)),
            pl.BlockSpec((1, T, D_MODEL), lambda b, t: (b, t, 0)),
        ),
        compiler_params=_cparams(("parallel", "parallel")),
        name="mla_pre",
    )(x, mod_i, g, pos_col, invf, win, qg, wq, kvg, wkv)


MLA_HEADS_PER_STEP = 2


def _key_slice(t):
    return pl.ds(pl.multiple_of(t * K_TILE, K_TILE), K_TILE)


def _key_scores(k_ref, qt_ref, hh, t):
    return _dot(k_ref[0, hh, _key_slice(t), :], qt_ref[0, hh])


def _mla_attn_kernel(qt_ref, k_ref, vt_ref, sg_ref, o_ref, s_ref, m_ref, acc_ref):
    n_kv = k_ref.shape[2] // K_TILE
    m_ref[...] = jnp.full(m_ref.shape, -jnp.inf, F32)
    acc_ref[...] = jnp.zeros(acc_ref.shape, F32)

    heads = range(MLA_HEADS_PER_STEP)

    def scores_fn(t):
        return [functools.partial(_key_scores, k_ref, qt_ref, hh, t) for hh in heads]

    def values_fn(t):
        return [vt_ref[0, hh, :, _key_slice(t)] for hh in heads]

    _pipelined_key_loop(n_kv, scores_fn, values_fn, s_ref, m_ref, acc_ref)

    outs = []
    for hh in range(MLA_HEADS_PER_STEP):
        acc = acc_ref[hh]
        outs.append(acc[:MLA_V] / acc[MLA_V:MLA_V + 1])
    o = jnp.concatenate(outs, axis=0).T
    o_ref[0] = (o * sg_ref[0]).astype(BF16)


def _mla_attn(qt, k, vt, sg):
    B, H, S, _ = k.shape
    G = MLA_HEADS_PER_STEP
    return pl.pallas_call(
        _mla_attn_kernel,
        out_shape=jax.ShapeDtypeStruct((B, S, H * MLA_V), BF16),
        grid=(B, H // G, S // Q_TILE),
        in_specs=[
            pl.BlockSpec((1, G, HEAD_PAD, Q_TILE), lambda b, h, i: (b, h, 0, i)),
            pl.BlockSpec((1, G, S, HEAD_PAD), lambda b, h, i: (b, h, 0, 0)),
            pl.BlockSpec((1, G, MLA_VT_ROWS, S), lambda b, h, i: (b, h, 0, 0)),
            pl.BlockSpec((1, Q_TILE, G * MLA_V), lambda b, h, i: (b, i, h)),
        ],
        out_specs=pl.BlockSpec((1, Q_TILE, G * MLA_V), lambda b, h, i: (b, i, h)),
        scratch_shapes=[
            pltpu.VMEM((2, G, K_TILE, Q_TILE), F32),
            pltpu.VMEM((G, 1, Q_TILE), F32),
            pltpu.VMEM((G, MLA_VT_ROWS, Q_TILE), F32),
        ],
        compiler_params=_cparams(("parallel", "parallel", "parallel")),
        name="mla_attn",
    )(qt, k, vt, sg)


def _diff_pre_kernel(x_ref, mod_ref, g_ref, win_ref, qt_out, k_out, vt_out, sg_out):
    h = _modulated_norm(x_ref, mod_ref, g_ref).astype(BF16)
    proj = _dot(h, win_ref[...])
    qscale = (DIFF_HD ** -0.5) * LOG2E
    sg_out[0] = _silu(proj[:, 3 * DIFF_W:])
    T = proj.shape[0]
    sub = lax.broadcasted_iota(jnp.int32, (BF16_SUBLANES, T), 0)
    ones_rows = (sub == 0).astype(F32)
    for hd in range(DIFF_HEADS):
        a = hd * HEAD_PAD
        b = a + HEAD_PAD
        qt_out[0, hd] = (proj[:, a:b] * qscale).T.astype(BF16)
        k_out[0, hd] = proj[:, DIFF_W + a:DIFF_W + b].astype(BF16)
        v_t = proj[:, 2 * DIFF_W + a:2 * DIFF_W + b].T
        vt_out[0, hd] = jnp.concatenate([v_t, ones_rows], axis=0).astype(BF16)


def _diff_pre(x, mod_i, g, win):
    B, S, _ = x.shape
    T = ROW_TILE
    const = lambda shape: pl.BlockSpec(shape, lambda b, t: (0,) * len(shape))
    return pl.pallas_call(
        _diff_pre_kernel,
        out_shape=(
            jax.ShapeDtypeStruct((B, DIFF_HEADS, HEAD_PAD, S), BF16),
            jax.ShapeDtypeStruct((B, DIFF_HEADS, S, HEAD_PAD), BF16),
            jax.ShapeDtypeStruct((B, DIFF_HEADS, DIFF_VT_ROWS, S), BF16),
            jax.ShapeDtypeStruct((B, S, D_MODEL), F32),
        ),
        grid=(B, S // T),
        in_specs=[
            pl.BlockSpec((1, T, D_MODEL), lambda b, t: (b, t, 0)),
            pl.BlockSpec((1, 1, 3 * D_MODEL), lambda b, t: (b, 0, 0)),
            const((1, D_MODEL)),
            const(win.shape),
        ],
        out_specs=(
            pl.BlockSpec((1, DIFF_HEADS, HEAD_PAD, T), lambda b, t: (b, 0, 0, t)),
            pl.BlockSpec((1, DIFF_HEADS, T, HEAD_PAD), lambda b, t: (b, 0, t, 0)),
            pl.BlockSpec((1, DIFF_HEADS, DIFF_VT_ROWS, T), lambda b, t: (b, 0, 0, t)),
            pl.BlockSpec((1, T, D_MODEL), lambda b, t: (b, t, 0)),
        ),
        compiler_params=_cparams(("parallel", "parallel")),
        name="diff_pre",
    )(x, mod_i, g, win)


def _diff_attn_kernel(qt_ref, k_ref, vt_ref, sg_ref, posq_ref, posk_ref, slope_ref,
                      lq1_ref, lk1_ref, lq2_ref, lk2_ref, hg_ref, o_ref,
                      s_ref, m_ref, acc_ref, *, lambda_init):
    n_kv = k_ref.shape[2] // K_TILE
    m_ref[...] = jnp.full(m_ref.shape, -jnp.inf, F32)
    acc_ref[...] = jnp.zeros(acc_ref.shape, F32)

    qt = qt_ref[0, 0]
    row = lax.broadcasted_iota(jnp.int32, qt.shape, 0)
    zero = jnp.zeros_like(qt)
    qt_maps = (jnp.where(row < DIFF_HD, qt, zero), jnp.where(row >= DIFF_HD, qt, zero))
    slope = slope_ref[0][:, 0:1]
    pos_q = posq_ref[0].astype(F32) * slope

    def scores_fn(t):
        pos_k = posk_ref[0, _key_slice(t), :].astype(F32) * slope
        bias = jnp.abs(pos_k - pos_q)
        kk = k_ref[0, 0, _key_slice(t), :]
        return [lambda qt_c=qt_c: _dot(kk, qt_c) - bias for qt_c in qt_maps]

    def values_fn(t):
        vt = vt_ref[0, 0, :, _key_slice(t)]
        return [vt, vt]

    _pipelined_key_loop(n_kv, scores_fn, values_fn, s_ref, m_ref, acc_ref)

    lam = (jnp.exp(jnp.sum(lq1_ref[...] * lk1_ref[...], axis=-1, keepdims=True))
           - jnp.exp(jnp.sum(lq2_ref[...] * lk2_ref[...], axis=-1, keepdims=True))
           + lambda_init)
    a1 = acc_ref[0]
    a2 = acc_ref[1]
    o = a1[:DIFF_V] / a1[DIFF_V:DIFF_V + 1] - lam * (a2[:DIFF_V] / a2[DIFF_V:DIFF_V + 1])
    o = o * lax.rsqrt(jnp.mean(o * o, axis=0, keepdims=True) + EPS)
    o = o.T * (hg_ref[...] * (1.0 - lambda_init))
    o_ref[0] = (o * sg_ref[0]).astype(BF16)


def _diff_attn(qt, k, vt, sg, pos_col, pos_row, slopes, lq1, lk1, lq2, lk2, hg, lambda_init):
    B, H, S, _ = k.shape
    vec = lambda n: pl.BlockSpec((1, n), lambda b, h, i: (0, 0))
    return pl.pallas_call(
        functools.partial(_diff_attn_kernel, lambda_init=lambda_init),
        out_shape=jax.ShapeDtypeStruct((B, S, DIFF_W), BF16),
        grid=(B, H, S // Q_TILE),
        in_specs=[
            pl.BlockSpec((1, 1, HEAD_PAD, Q_TILE), lambda b, h, i: (b, h, 0, i)),
            pl.BlockSpec((1, 1, S, HEAD_PAD), lambda b, h, i: (b, h, 0, 0)),
            pl.BlockSpec((1, 1, DIFF_VT_ROWS, S), lambda b, h, i: (b, h, 0, 0)),
            pl.BlockSpec((1, Q_TILE, HEAD_PAD), lambda b, h, i: (b, i, h)),
            pl.BlockSpec((1, 1, Q_TILE), lambda b, h, i: (b, 0, i)),
            pl.BlockSpec((1, S, 1), lambda b, h, i: (b, 0, 0)),
            pl.BlockSpec((1, 1, LANES), lambda b, h, i: (h, 0, 0)),
            vec(DIFF_HD), vec(DIFF_HD), vec(DIFF_HD), vec(DIFF_HD),
            vec(DIFF_V),
        ],
        out_specs=pl.BlockSpec((1, Q_TILE, HEAD_PAD), lambda b, h, i: (b, i, h)),
        scratch_shapes=[
            pltpu.VMEM((2, 2, K_TILE, Q_TILE), F32),
            pltpu.VMEM((2, 1, Q_TILE), F32),
            pltpu.VMEM((2, DIFF_VT_ROWS, Q_TILE), F32),
        ],
        compiler_params=_cparams(("parallel", "parallel", "parallel")),
        name="diff_attn",
    )(qt, k, vt, sg, pos_row, pos_col, slopes, lq1, lk1, lq2, lk2, hg)


def _post_kernel(o_ref, x_ref, mod_ref, wo_ref, fg_ref, out_ref, *, final):
    y = _dot(o_ref[0], wo_ref[...])
    gate = mod_ref[0, :, 2 * D_MODEL:]
    x_new = x_ref[0] + gate * y
    if final:
        x_new = _rms(x_new, fg_ref[...])
    out_ref[0] = x_new


def _post(o, x, mod_i, wo, fg, final):
    B, S, _ = x.shape
    T = ROW_TILE
    const = lambda shape: pl.BlockSpec(shape, lambda b, t: (0,) * len(shape))
    return pl.pallas_call(
        functools.partial(_post_kernel, final=final),
        out_shape=jax.ShapeDtypeStruct((B, S, D_MODEL), F32),
        grid=(B, S // T),
        in_specs=[
            pl.BlockSpec((1, T, D_MODEL), lambda b, t: (b, t, 0)),
            pl.BlockSpec((1, T, D_MODEL), lambda b, t: (b, t, 0)),
            pl.BlockSpec((1, 1, 3 * D_MODEL), lambda b, t: (b, 0, 0)),
            const(wo.shape),
            const((1, D_MODEL)),
        ],
        out_specs=pl.BlockSpec((1, T, D_MODEL), lambda b, t: (b, t, 0)),
        compiler_params=_cparams(("parallel", "parallel")),
        name="post_final" if final else "post",
    )(o, x, mod_i, wo, fg)


def _rot_cols(w):
    half = MLA_ROPE // 2
    return jnp.concatenate([-w[..., half:], w[..., :half]], axis=-1)


def _pad_slab(w, lo):
    pad = [(0, 0)] * (w.ndim - 1) + [(lo, HEAD_PAD - lo - w.shape[-1])]
    return jnp.pad(w, pad)


def _mla_weights(w_in, w_q_up, w_kv_up):
    o1 = MLA_Q_RANK
    o2 = o1 + MLA_KV_RANK
    o3 = o2 + MLA_ROPE
    w_kr = w_in[:, o2:o3]
    win = jnp.concatenate([
        w_in[:, :o2],
        _pad_slab(w_kr, MLA_NOPE),
        _pad_slab(_rot_cols(w_kr), MLA_NOPE),
        w_in[:, o3:],
    ], axis=1).astype(BF16)
    wq3 = w_q_up.reshape(MLA_Q_RANK, MLA_HEADS, MLA_QK)
    wq_plain = _pad_slab(wq3, 0).reshape(MLA_Q_RANK, MLA_HEADS * HEAD_PAD)
    wq_rot = _pad_slab(_rot_cols(wq3[..., MLA_NOPE:]), MLA_NOPE).reshape(MLA_Q_RANK, MLA_HEADS * HEAD_PAD)
    wq = jnp.concatenate([wq_plain, wq_rot], axis=1).astype(BF16)
    wkv3 = w_kv_up.reshape(MLA_KV_RANK, MLA_HEADS, MLA_NOPE + MLA_V)
    wk = _pad_slab(wkv3[..., :MLA_NOPE], 0).reshape(MLA_KV_RANK, MLA_HEADS * HEAD_PAD)
    wv = _pad_slab(wkv3[..., MLA_NOPE:], 0).reshape(MLA_KV_RANK, MLA_HEADS * HEAD_PAD)
    wkv = jnp.concatenate([wk, wv], axis=1).astype(BF16)
    return win, wq, wkv


def kernel(x, c, positions, ada_w, ada_b, norm_g, mla_w_in, mla_q_norm_g, mla_w_q_up, mla_kv_norm_g,
           mla_w_kv_up, mla_w_o, diff_w_in, diff_lq1, diff_lk1, diff_lq2, diff_lk2, diff_head_g,
           diff_w_o, final_g):
    B, S, D = x.shape
    assert D == D_MODEL and S % Q_TILE == 0 and S % K_TILE == 0 and S % ROW_TILE == 0
    c_pad = jnp.pad(c, ((0, 8 - B), (0, 0)))
    mod = _ada_mod(c_pad, ada_w, ada_b)[:, :B].reshape(DEPTH, B, 1, 3 * D)

    pos_col = positions.reshape(B, S, 1)
    pos_row = positions.reshape(B, 1, S)
    inv = ROPE_BASE ** (-jnp.arange(0, MLA_ROPE, 2, dtype=F32) / MLA_ROPE)
    invf = _pad_slab(jnp.concatenate([inv, inv]), MLA_NOPE).reshape(1, HEAD_PAD)
    slopes = jnp.exp2(-8.0 * jnp.arange(1, DIFF_HEADS + 1, dtype=F32) / DIFF_HEADS) * LOG2E
    slopes = jnp.broadcast_to(slopes[:, None, None], (DIFF_HEADS, 1, LANES))
    final_row = final_g.reshape(1, D)

    for i in range(DEPTH):
        j = i // 2
        g = norm_g[i].reshape(1, D)
        last = i == DEPTH - 1
        if i % 2 == 0:
            win, wq, wkv = _mla_weights(mla_w_in[j], mla_w_q_up[j], mla_w_kv_up[j])
            qt, k, vt, sg = _mla_pre(x, mod[i], g, pos_col, invf, win,
                                     mla_q_norm_g[j].reshape(1, -1), wq,
                                     mla_kv_norm_g[j].reshape(1, -1), wkv)
            o = _mla_attn(qt, k, vt, sg)
            wo = mla_w_o[j].astype(BF16)
        else:
            lambda_init = 0.8 - 0.6 * math.exp(-0.3 * i)
            qt, k, vt, sg = _diff_pre(x, mod[i], g, diff_w_in[j].astype(BF16))
            o = _diff_attn(qt, k, vt, sg, pos_col, pos_row, slopes,
                           diff_lq1[j].reshape(1, -1), diff_lk1[j].reshape(1, -1),
                           diff_lq2[j].reshape(1, -1), diff_lk2[j].reshape(1, -1),
                           diff_head_g[j].reshape(1, -1), lambda_init)
            wo = diff_w_o[j].astype(BF16)
        x = _post(o, x, mod[i], wo, final_row, last)
    return x
```

```python
import functools
import math

import jax
import jax.numpy as jnp
from jax import lax
from jax.experimental import pallas as pl
from jax.experimental.pallas import tpu as pltpu

D_MODEL = 1024
DEPTH = 4
EPS = 1e-6

MLA_HEADS = 16
MLA_NOPE = 64
MLA_ROPE = 32
MLA_V = 64
MLA_Q_RANK = 384
MLA_KV_RANK = 256
ROPE_BASE = 10000.0
MLA_QK = MLA_NOPE + MLA_ROPE

DIFF_HD = 64
DIFF_HEADS = D_MODEL // (2 * DIFF_HD)
DIFF_W = DIFF_HEADS * 2 * DIFF_HD
DIFF_V = 2 * DIFF_HD

LANES = 128
BF16_SUBLANES = 16
HEAD_PAD = LANES
LOG2E = math.log2(math.e)

MLA_VT_ROWS = MLA_V + BF16_SUBLANES
DIFF_VT_ROWS = DIFF_V + BF16_SUBLANES

ROW_TILE = 256
MLA_Q_TILE = 1024
MLA_K_TILE = 512
DIFF_Q_TILE = 512
DIFF_K_TILE = 256
MLA_STEPS_PER_TRIP = 4
DIFF_STEPS_PER_TRIP = 2
PRUNE_LOG2 = 150.0
NORM_MARGIN = 1.02
VMEM_LIMIT = 56 * 1024 * 1024

BF16 = jnp.bfloat16
F32 = jnp.float32


def _cparams(sem):
    return pltpu.CompilerParams(dimension_semantics=sem, vmem_limit_bytes=VMEM_LIMIT)


def _dot(a, b):
    return jnp.dot(a, b, preferred_element_type=F32)


def _rms(x, g):
    return x * lax.rsqrt(jnp.mean(x * x, axis=-1, keepdims=True) + EPS) * g


def _silu(x):
    return x * (1.0 / (1.0 + jnp.exp(-x)))


def _mod_kernel(c_ref, w_ref, b_ref, o_ref):
    c = c_ref[...]
    o_ref[0] = _dot(_silu(c).astype(BF16), w_ref[0].astype(BF16)) + b_ref[0]


def _ada_mod(c_pad, ada_w, ada_b):
    rows = c_pad.shape[0]
    nblk = 3
    return pl.pallas_call(
        _mod_kernel,
        out_shape=jax.ShapeDtypeStruct((DEPTH, rows, 3 * D_MODEL), F32),
        grid=(DEPTH, nblk),
        in_specs=[
            pl.BlockSpec((rows, D_MODEL), lambda i, n: (0, 0)),
            pl.BlockSpec((1, D_MODEL, D_MODEL), lambda i, n: (i, 0, n)),
            pl.BlockSpec((1, 1, D_MODEL), lambda i, n: (i, 0, n)),
        ],
        out_specs=pl.BlockSpec((1, rows, D_MODEL), lambda i, n: (i, 0, n)),
        compiler_params=_cparams(("arbitrary", "arbitrary")),
        name="ada_mod",
    )(c_pad, ada_w, ada_b.reshape(DEPTH, 1, 3 * D_MODEL))


def _modulated_norm(x_ref, mod_ref, g_ref):
    x = x_ref[0]
    shift = mod_ref[0, :, 0:D_MODEL]
    scale = mod_ref[0, :, D_MODEL:2 * D_MODEL]
    return _rms(x, g_ref[...]) * (1.0 + scale) + shift


def _pipelined_key_loop(scores_fn, values_fn, s_ref, m_ref, acc_ref, *, steps_per_trip,
                        first_trip, last_trip):
    def step(t_cons, slot, col_max, t_prod):
        thunks = scores_fn(t_prod) if t_prod is not None else None
        next_max = []
        for c, vt in enumerate(values_fn(t_cons)):
            if thunks is not None:
                s = thunks[c]()
                s_ref[1 - slot, c] = s
                next_max.append(jnp.max(s, axis=0, keepdims=True))
            m_prev = m_ref[c]
            m_new = jnp.maximum(m_prev, col_max[c])
            p = jnp.exp2(s_ref[slot, c] - m_new).astype(BF16)
            alpha = jnp.exp2(m_prev - m_new)
            m_ref[c] = m_new
            acc_ref[c] = alpha * acc_ref[c] + _dot(vt, p)
        return tuple(next_max)

    def trip(i, col_max, last):
        t0 = steps_per_trip * i
        for u in range(steps_per_trip):
            final_step = last and u == steps_per_trip - 1
            col_max = step(t0 + u, u % 2, col_max, None if final_step else t0 + u + 1)
        return col_max

    assert steps_per_trip % 2 == 0
    cm = []
    for c, thunk in enumerate(scores_fn(steps_per_trip * first_trip)):
        s = thunk()
        s_ref[0, c] = s
        cm.append(jnp.max(s, axis=0, keepdims=True))
    cm = lax.fori_loop(first_trip, last_trip, lambda i, cm: trip(i, cm, False), tuple(cm))
    trip(last_trip, cm, True)


def _mla_pre_kernel(x_ref, mod_ref, g_ref, pos_ref, invf_ref, win_ref, qg_ref, wq_ref,
                    kvg_ref, wkv_ref, qt_out, k_out, vt_out, sg_out):
    h = _modulated_norm(x_ref, mod_ref, g_ref).astype(BF16)
    proj = _dot(h, win_ref[...])
    o1 = MLA_Q_RANK
    o2 = o1 + MLA_KV_RANK
    o3 = o2 + HEAD_PAD
    o4 = o3 + HEAD_PAD
    qn = _rms(proj[:, :o1], qg_ref[...]).astype(BF16)
    kvn = _rms(proj[:, o1:o2], kvg_ref[...]).astype(BF16)
    ang = pos_ref[0].astype(F32) * invf_ref[...]
    cos = jnp.cos(ang)
    sin = jnp.sin(ang)
    k_rope = proj[:, o2:o3] * cos + proj[:, o3:o4] * sin
    sg_out[0] = _silu(proj[:, o4:])

    qq = _dot(qn, wq_ref[...])
    kv = _dot(kvn, wkv_ref[...])
    half = MLA_HEADS * HEAD_PAD
    qscale = (MLA_QK ** -0.5) * LOG2E
    lane = lax.broadcasted_iota(jnp.int32, (1, HEAD_PAD), 1)
    ones_col = (lane == MLA_V).astype(F32)
    for hd in range(MLA_HEADS):
        a = hd * HEAD_PAD
        b = a + HEAD_PAD
        q_h = (qq[:, a:b] * cos + qq[:, half + a:half + b] * sin) * qscale
        qt_out[0, hd] = q_h.T.astype(BF16)
        k_out[0, hd] = (kv[:, a:b] + k_rope).astype(BF16)
        v_h = kv[:, half + a:half + b] + ones_col
        vt_out[0, hd] = v_h.T[:MLA_VT_ROWS].astype(BF16)


def _mla_pre(x, mod_i, g, pos_col, invf, win, qg, wq, kvg, wkv):
    B, S, _ = x.shape
    T = ROW_TILE
    const = lambda shape: pl.BlockSpec(shape, lambda b, t: (0,) * len(shape))
    return pl.pallas_call(
        _mla_pre_kernel,
        out_shape=(
            jax.ShapeDtypeStruct((B, MLA_HEADS, HEAD_PAD, S), BF16),
            jax.ShapeDtypeStruct((B, MLA_HEADS, S, HEAD_PAD), BF16),
            jax.ShapeDtypeStruct((B, MLA_HEADS, MLA_VT_ROWS, S), BF16),
            jax.ShapeDtypeStruct((B, S, D_MODEL), F32),
        ),
        grid=(B, S // T),
        in_specs=[
            pl.BlockSpec((1, T, D_MODEL), lambda b, t: (b, t, 0)),
            pl.BlockSpec((1, 1, 3 * D_MODEL), lambda b, t: (b, 0, 0)),
            const((1, D_MODEL)),
            pl.BlockSpec((1, T, 1), lambda b, t: (b, t, 0)),
            const((1, HEAD_PAD)),
            const(win.shape),
            const((1, MLA_Q_RANK)),
            const(wq.shape),
            const((1, MLA_KV_RANK)),
            const(wkv.shape),
        ],
        out_specs=(
            pl.BlockSpec((1, MLA_HEADS, HEAD_PAD, T), lambda b, t: (b, 0, 0, t)),
            pl.BlockSpec((1, MLA_HEADS, T, HEAD_PAD), lambda b, t: (b, 0, t, 0)),
            pl.BlockSpec((1, MLA_HEADS, MLA_VT_ROWS, T), lambda b, t: (b, 0, 0, t)),
            pl.BlockSpec((1, T, D_MODEL), lambda b, t: (b, t, 0)),
        ),
        compiler_params=_cparams(("parallel", "parallel")),
        name="mla_pre",
    )(x, mod_i, g, pos_col, invf, win, qg, wq, kvg, wkv)


MLA_HEADS_PER_STEP = 2


def _key_slice(t, tile):
    return pl.ds(pl.multiple_of(t * tile, tile), tile)


def _key_scores(k_ref, qt_ref, hh, t):
    return _dot(k_ref[0, hh, _key_slice(t, MLA_K_TILE), :], qt_ref[0, hh])


def _mla_attn_kernel(qt_ref, k_ref, vt_ref, sg_ref, o_ref, s_ref, m_ref, acc_ref):
    n_kv = k_ref.shape[2] // MLA_K_TILE
    m_ref[...] = jnp.full(m_ref.shape, -jnp.inf, F32)
    acc_ref[...] = jnp.zeros(acc_ref.shape, F32)

    heads = range(MLA_HEADS_PER_STEP)

    def scores_fn(t):
        return [functools.partial(_key_scores, k_ref, qt_ref, hh, t) for hh in heads]

    def values_fn(t):
        return [vt_ref[0, hh, :, _key_slice(t, MLA_K_TILE)] for hh in heads]

    assert n_kv % MLA_STEPS_PER_TRIP == 0
    _pipelined_key_loop(scores_fn, values_fn, s_ref, m_ref, acc_ref,
                        steps_per_trip=MLA_STEPS_PER_TRIP, first_trip=0,
                        last_trip=n_kv // MLA_STEPS_PER_TRIP - 1)

    outs = []
    for hh in range(MLA_HEADS_PER_STEP):
        acc = acc_ref[hh]
        outs.append(acc[:MLA_V] / acc[MLA_V:MLA_V + 1])
    o = jnp.concatenate(outs, axis=0).T
    o_ref[0] = (o * sg_ref[0]).astype(BF16)


def _mla_attn(qt, k, vt, sg):
    B, H, S, _ = k.shape
    G = MLA_HEADS_PER_STEP
    return pl.pallas_call(
        _mla_attn_kernel,
        out_shape=jax.ShapeDtypeStruct((B, S, H * MLA_V), BF16),
        grid=(B, H // G, S // MLA_Q_TILE),
        in_specs=[
            pl.BlockSpec((1, G, HEAD_PAD, MLA_Q_TILE), lambda b, h, i: (b, h, 0, i)),
            pl.BlockSpec((1, G, S, HEAD_PAD), lambda b, h, i: (b, h, 0, 0)),
            pl.BlockSpec((1, G, MLA_VT_ROWS, S), lambda b, h, i: (b, h, 0, 0)),
            pl.BlockSpec((1, MLA_Q_TILE, G * MLA_V), lambda b, h, i: (b, i, h)),
        ],
        out_specs=pl.BlockSpec((1, MLA_Q_TILE, G * MLA_V), lambda b, h, i: (b, i, h)),
        scratch_shapes=[
            pltpu.VMEM((2, G, MLA_K_TILE, MLA_Q_TILE), F32),
            pltpu.VMEM((G, 1, MLA_Q_TILE), F32),
            pltpu.VMEM((G, MLA_VT_ROWS, MLA_Q_TILE), F32),
        ],
        compiler_params=_cparams(("parallel", "parallel", "parallel")),
        name="mla_attn",
    )(qt, k, vt, sg)


def _diff_pre_kernel(x_ref, mod_ref, g_ref, win_ref, qt_out, k_out, vt_out, sg_out, kn_out):
    h = _modulated_norm(x_ref, mod_ref, g_ref).astype(BF16)
    proj = _dot(h, win_ref[...])
    qscale = (DIFF_HD ** -0.5) * LOG2E
    sg_out[0] = _silu(proj[:, 3 * DIFF_W:])
    T = proj.shape[0]
    sub = lax.broadcasted_iota(jnp.int32, (BF16_SUBLANES, T), 0)
    ones_rows = (sub == 0).astype(F32)
    k_sq_max = []
    for hd in range(DIFF_HEADS):
        a = hd * HEAD_PAD
        b = a + HEAD_PAD
        qt_out[0, hd] = (proj[:, a:b] * qscale).T.astype(BF16)
        k_h = proj[:, DIFF_W + a:DIFF_W + b]
        k_out[0, hd] = k_h.astype(BF16)
        k_sq = jnp.sum(k_h * k_h, axis=-1, keepdims=True)
        k_sq_max.append(jnp.broadcast_to(jnp.max(k_sq, axis=0, keepdims=True), (1, LANES)))
        v_t = proj[:, 2 * DIFF_W + a:2 * DIFF_W + b].T
        vt_out[0, hd] = jnp.concatenate([v_t, ones_rows], axis=0).astype(BF16)
    kn_out[0, 0] = jnp.concatenate(k_sq_max, axis=0)


def _diff_pre(x, mod_i, g, win):
    B, S, _ = x.shape
    T = ROW_TILE
    const = lambda shape: pl.BlockSpec(shape, lambda b, t: (0,) * len(shape))
    return pl.pallas_call(
        _diff_pre_kernel,
        out_shape=(
            jax.ShapeDtypeStruct((B, DIFF_HEADS, HEAD_PAD, S), BF16),
            jax.ShapeDtypeStruct((B, DIFF_HEADS, S, HEAD_PAD), BF16),
            jax.ShapeDtypeStruct((B, DIFF_HEADS, DIFF_VT_ROWS, S), BF16),
            jax.ShapeDtypeStruct((B, S, D_MODEL), F32),
            jax.ShapeDtypeStruct((B, S // T, DIFF_HEADS, LANES), F32),
        ),
        grid=(B, S // T),
        in_specs=[
            pl.BlockSpec((1, T, D_MODEL), lambda b, t: (b, t, 0)),
            pl.BlockSpec((1, 1, 3 * D_MODEL), lambda b, t: (b, 0, 0)),
            const((1, D_MODEL)),
            const(win.shape),
        ],
        out_specs=(
            pl.BlockSpec((1, DIFF_HEADS, HEAD_PAD, T), lambda b, t: (b, 0, 0, t)),
            pl.BlockSpec((1, DIFF_HEADS, T, HEAD_PAD), lambda b, t: (b, 0, t, 0)),
            pl.BlockSpec((1, DIFF_HEADS, DIFF_VT_ROWS, T), lambda b, t: (b, 0, 0, t)),
            pl.BlockSpec((1, T, D_MODEL), lambda b, t: (b, t, 0)),
            pl.BlockSpec((1, 1, DIFF_HEADS, LANES), lambda b, t: (b, t, 0, 0)),
        ),
        compiler_params=_cparams(("parallel", "parallel")),
        name="diff_pre",
    )(x, mod_i, g, win)


def _needed_trips(qt, posq, posrow_ref, kn_ref, slope, n_trips, keys_per_trip):
    qf = qt.astype(F32)
    q_sq = jnp.max(jnp.sum(qf * qf, axis=0, keepdims=True))
    score_bound = jnp.sqrt(q_sq * jnp.max(kn_ref[0])) * NORM_MARGIN
    limit = PRUNE_LOG2 + 2.0 * score_bound
    q_lo = jnp.min(posq)
    q_hi = jnp.max(posq)
    first = jnp.int32(n_trips - 1)
    last = jnp.int32(0)
    for p in range(n_trips):
        pk = posrow_ref[0, :, p * keys_per_trip:(p + 1) * keys_per_trip]
        gap = jnp.maximum(jnp.maximum(jnp.min(pk) - q_hi, q_lo - jnp.max(pk)), 0)
        needed = gap.astype(F32) * slope <= limit
        first = jnp.where(needed, jnp.minimum(first, p), first)
        last = jnp.where(needed, jnp.maximum(last, p), last)
    return first, jnp.maximum(last, first)


def _diff_attn_kernel(qt_ref, k_ref, vt_ref, sg_ref, posq_ref, posk_ref, posrow_ref, kn_ref,
                      slope_ref, lq1_ref, lk1_ref, lq2_ref, lk2_ref, hg_ref, o_ref,
                      s_ref, m_ref, acc_ref, *, lambda_init):
    n_kv = k_ref.shape[2] // DIFF_K_TILE
    m_ref[...] = jnp.full(m_ref.shape, -jnp.inf, F32)
    acc_ref[...] = jnp.zeros(acc_ref.shape, F32)

    qt = qt_ref[0, 0]
    row = lax.broadcasted_iota(jnp.int32, qt.shape, 0)
    zero = jnp.zeros_like(qt)
    qt_maps = (jnp.where(row < DIFF_HD, qt, zero), jnp.where(row >= DIFF_HD, qt, zero))
    slope = slope_ref[0][:, 0:1]
    pos_q = posq_ref[0].astype(F32) * slope

    def scores_fn(t):
        keys = _key_slice(t, DIFF_K_TILE)
        pos_k = posk_ref[0, keys, :].astype(F32) * slope
        bias = jnp.abs(pos_k - pos_q)
        kk = k_ref[0, 0, keys, :]
        return [lambda qt_c=qt_c: _dot(kk, qt_c) - bias for qt_c in qt_maps]

    def values_fn(t):
        vt = vt_ref[0, 0, :, _key_slice(t, DIFF_K_TILE)]
        return [vt, vt]

    assert n_kv % DIFF_STEPS_PER_TRIP == 0
    first_trip, last_trip = _needed_trips(
        qt, posq_ref[0], posrow_ref, kn_ref, jnp.max(slope_ref[0]),
        n_kv // DIFF_STEPS_PER_TRIP, DIFF_STEPS_PER_TRIP * DIFF_K_TILE)
    _pipelined_key_loop(scores_fn, values_fn, s_ref, m_ref, acc_ref,
                        steps_per_trip=DIFF_STEPS_PER_TRIP, first_trip=first_trip,
                        last_trip=last_trip)

    lam =(jnp.exp(jnp.sum(lq1_ref[...] * lk1_ref[...], axis=-1, keepdims=True))
           - jnp.exp(jnp.sum(lq2_ref[...] * lk2_ref[...], axis=-1, keepdims=True))
           + lambda_init)
    a1 = acc_ref[0]
    a2 = acc_ref[1]
    o = a1[:DIFF_V] / a1[DIFF_V:DIFF_V + 1] - lam * (a2[:DIFF_V] / a2[DIFF_V:DIFF_V + 1])
    o = o * lax.rsqrt(jnp.mean(o * o, axis=0, keepdims=True) + EPS)
    o = o.T * (hg_ref[...] * (1.0 - lambda_init))
    o_ref[0] = (o * sg_ref[0]).astype(BF16)


def _diff_attn(qt, k, vt, sg, pos_col, pos_row, k_sq_max, slopes, lq1, lk1, lq2, lk2, hg, lambda_init):
    B, H, S, _ = k.shape
    vec = lambda n: pl.BlockSpec((1, n), lambda b, h, i: (0, 0))
    return pl.pallas_call(
        functools.partial(_diff_attn_kernel, lambda_init=lambda_init),
        out_shape=jax.ShapeDtypeStruct((B, S, DIFF_W), BF16),
        grid=(B, H, S // DIFF_Q_TILE),
        in_specs=[
            pl.BlockSpec((1, 1, HEAD_PAD, DIFF_Q_TILE), lambda b, h, i: (b, h, 0, i)),
            pl.BlockSpec((1, 1, S, HEAD_PAD), lambda b, h, i: (b, h, 0, 0)),
            pl.BlockSpec((1, 1, DIFF_VT_ROWS, S), lambda b, h, i: (b, h, 0, 0)),
            pl.BlockSpec((1, DIFF_Q_TILE, HEAD_PAD), lambda b, h, i: (b, i, h)),
            pl.BlockSpec((1, 1, DIFF_Q_TILE), lambda b, h, i: (b, 0, i)),
            pl.BlockSpec((1, S, 1), lambda b, h, i: (b, 0, 0)),
            pl.BlockSpec((1, 1, S), lambda b, h, i: (b, 0, 0)),
            pl.BlockSpec((1,) + k_sq_max.shape[1:], lambda b, h, i: (b, 0, 0, 0)),
            pl.BlockSpec((1, 1, LANES), lambda b, h, i: (h, 0, 0)),
            vec(DIFF_HD), vec(DIFF_HD), vec(DIFF_HD), vec(DIFF_HD),
            vec(DIFF_V),
        ],
        out_specs=pl.BlockSpec((1, DIFF_Q_TILE, HEAD_PAD), lambda b, h, i: (b, i, h)),
        scratch_shapes=[
            pltpu.VMEM((2, 2, DIFF_K_TILE, DIFF_Q_TILE), F32),
            pltpu.VMEM((2, 1, DIFF_Q_TILE), F32),
            pltpu.VMEM((2, DIFF_VT_ROWS, DIFF_Q_TILE), F32),
        ],
        compiler_params=_cparams(("parallel", "parallel", "parallel")),
        name="diff_attn",
    )(qt, k, vt, sg, pos_row, pos_col, pos_row, k_sq_max, slopes, lq1, lk1, lq2, lk2, hg)


def _post_kernel(o_ref, x_ref, mod_ref, wo_ref, fg_ref, out_ref, *, final):
    y = _dot(o_ref[0], wo_ref[...])
    gate = mod_ref[0, :, 2 * D_MODEL:]
    x_new = x_ref[0] + gate * y
    if final:
        x_new = _rms(x_new, fg_ref[...])
    out_ref[0] = x_new


def _post(o, x, mod_i, wo, fg, final):
    B, S, _ = x.shape
    T = ROW_TILE
    const = lambda shape: pl.BlockSpec(shape, lambda b, t: (0,) * len(shape))
    return pl.pallas_call(
        functools.partial(_post_kernel, final=final),
        out_shape=jax.ShapeDtypeStruct((B, S, D_MODEL), F32),
        grid=(B, S // T),
        in_specs=[
            pl.BlockSpec((1, T, D_MODEL), lambda b, t: (b, t, 0)),
            pl.BlockSpec((1, T, D_MODEL), lambda b, t: (b, t, 0)),
            pl.BlockSpec((1, 1, 3 * D_MODEL), lambda b, t: (b, 0, 0)),
            const(wo.shape),
            const((1, D_MODEL)),
        ],
        out_specs=pl.BlockSpec((1, T, D_MODEL), lambda b, t: (b, t, 0)),
        compiler_params=_cparams(("parallel", "parallel")),
        name="post_final" if final else "post",
    )(o, x, mod_i, wo, fg)


def _rot_cols(w):
    half = MLA_ROPE // 2
    return jnp.concatenate([-w[..., half:], w[..., :half]], axis=-1)


def _pad_slab(w, lo):
    pad = [(0, 0)] * (w.ndim - 1) + [(lo, HEAD_PAD - lo - w.shape[-1])]
    return jnp.pad(w, pad)


def _mla_weights(w_in, w_q_up, w_kv_up):
    o1 = MLA_Q_RANK
    o2 = o1 + MLA_KV_RANK
    o3 = o2 + MLA_ROPE
    w_kr = w_in[:, o2:o3]
    win = jnp.concatenate([
        w_in[:, :o2],
        _pad_slab(w_kr, MLA_NOPE),
        _pad_slab(_rot_cols(w_kr), MLA_NOPE),
        w_in[:, o3:],
    ], axis=1).astype(BF16)
    wq3 = w_q_up.reshape(MLA_Q_RANK, MLA_HEADS, MLA_QK)
    wq_plain = _pad_slab(wq3, 0).reshape(MLA_Q_RANK, MLA_HEADS * HEAD_PAD)
    wq_rot = _pad_slab(_rot_cols(wq3[..., MLA_NOPE:]), MLA_NOPE).reshape(MLA_Q_RANK, MLA_HEADS * HEAD_PAD)
    wq = jnp.concatenate([wq_plain, wq_rot], axis=1).astype(BF16)
    wkv3 = w_kv_up.reshape(MLA_KV_RANK, MLA_HEADS, MLA_NOPE + MLA_V)
    wk = _pad_slab(wkv3[..., :MLA_NOPE], 0).reshape(MLA_KV_RANK, MLA_HEADS * HEAD_PAD)
    wv = _pad_slab(wkv3[..., MLA_NOPE:], 0).reshape(MLA_KV_RANK, MLA_HEADS * HEAD_PAD)
    wkv = jnp.concatenate([wk, wv], axis=1).astype(BF16)
    return win, wq, wkv


def kernel(x, c, positions, ada_w, ada_b, norm_g, mla_w_in, mla_q_norm_g, mla_w_q_up, mla_kv_norm_g,
           mla_w_kv_up, mla_w_o, diff_w_in, diff_lq1, diff_lk1, diff_lq2, diff_lk2, diff_head_g,
           diff_w_o, final_g):
    B, S, D = x.shape
    assert D == D_MODEL and S % ROW_TILE == 0
    assert S % MLA_Q_TILE == 0 and S % MLA_K_TILE == 0 and S % DIFF_Q_TILE == 0 and S % DIFF_K_TILE == 0
    c_pad = jnp.pad(c, ((0, 8 - B), (0, 0)))
    mod = _ada_mod(c_pad, ada_w, ada_b)[:, :B].reshape(DEPTH, B, 1, 3 * D)

    pos_col = positions.reshape(B, S, 1)
    pos_row = positions.reshape(B, 1, S)
    inv = ROPE_BASE ** (-jnp.arange(0, MLA_ROPE, 2, dtype=F32) / MLA_ROPE)
    invf = _pad_slab(jnp.concatenate([inv, inv]), MLA_NOPE).reshape(1, HEAD_PAD)
    slopes = jnp.exp2(-8.0 * jnp.arange(1, DIFF_HEADS + 1, dtype=F32) / DIFF_HEADS) * LOG2E
    slopes = jnp.broadcast_to(slopes[:, None, None], (DIFF_HEADS, 1, LANES))
    final_row = final_g.reshape(1, D)

    for i in range(DEPTH):
        j = i // 2
        g = norm_g[i].reshape(1, D)
        last = i == DEPTH - 1
        if i % 2 == 0:
            win, wq, wkv = _mla_weights(mla_w_in[j], mla_w_q_up[j], mla_w_kv_up[j])
            qt, k, vt, sg = _mla_pre(x, mod[i], g, pos_col, invf, win,
                                     mla_q_norm_g[j].reshape(1, -1), wq,
                                     mla_kv_norm_g[j].reshape(1, -1), wkv)
            o = _mla_attn(qt, k, vt, sg)
            wo = mla_w_o[j].astype(BF16)
        else:
            lambda_init = 0.8 - 0.6 * math.exp(-0.3 * i)
            qt, k, vt, sg, k_sq_max = _diff_pre(x, mod[i], g, diff_w_in[j].astype(BF16))
            o = _diff_attn(qt, k, vt, sg, pos_col, pos_row, k_sq_max, slopes,
                           diff_lq1[j].reshape(1, -1), diff_lk1[j].reshape(1, -1),
                           diff_lq2[j].reshape(1, -1), diff_lk2[j].reshape(1, -1),
                           diff_head_g[j].reshape(1, -1), lambda_init)
            wo = diff_w_o[j].astype(BF16)
        x = _post(o, x, mod[i], wo, final_row, last)
    return x
```

```python
import functools
import math

import jax
import jax.numpy as jnp
from jax import lax
from jax.experimental import pallas as pl
from jax.experimental.pallas import tpu as pltpu

D_MODEL = 1024
DEPTH = 4
EPS = 1e-6

MLA_HEADS = 16
MLA_NOPE = 64
MLA_ROPE = 32
MLA_V = 64
MLA_Q_RANK = 384
MLA_KV_RANK = 256
ROPE_BASE = 10000.0
MLA_QK = MLA_NOPE + MLA_ROPE

DIFF_HD = 64
DIFF_HEADS = D_MODEL // (2 * DIFF_HD)
DIFF_W = DIFF_HEADS * 2 * DIFF_HD
DIFF_V = 2 * DIFF_HD

LANES = 128
BF16_SUBLANES = 16
HEAD_PAD = LANES
LOG2E = math.log2(math.e)

MLA_VT_ROWS = MLA_V + BF16_SUBLANES
DIFF_VT_ROWS = DIFF_V + BF16_SUBLANES

ROW_TILE = 256
MLA_Q_TILE = 512
MLA_K_TILE = 512
DIFF_Q_TILE = 512
DIFF_K_TILE = 256
MLA_STEPS_PER_TRIP = 4
DIFF_STEPS_PER_TRIP = 2
PRUNE_LOG2 = 150.0
NORM_MARGIN = 1.02
VMEM_LIMIT = 56 * 1024 * 1024

BF16 = jnp.bfloat16
F32 = jnp.float32


def _cparams(sem):
    return pltpu.CompilerParams(dimension_semantics=sem, vmem_limit_bytes=VMEM_LIMIT)


def _dot(a, b):
    return jnp.dot(a, b, preferred_element_type=F32)


def _rms(x, g):
    return x * lax.rsqrt(jnp.mean(x * x, axis=-1, keepdims=True) + EPS) * g


def _silu(x):
    return x * (1.0 / (1.0 + jnp.exp(-x)))


def _mod_kernel(c_ref, w_ref, b_ref, o_ref):
    c = c_ref[...]
    o_ref[0] = _dot(_silu(c).astype(BF16), w_ref[0].astype(BF16)) + b_ref[0]


def _ada_mod(c_pad, ada_w, ada_b):
    rows = c_pad.shape[0]
    nblk = 3
    return pl.pallas_call(
        _mod_kernel,
        out_shape=jax.ShapeDtypeStruct((DEPTH, rows, 3 * D_MODEL), F32),
        grid=(DEPTH, nblk),
        in_specs=[
            pl.BlockSpec((rows, D_MODEL), lambda i, n: (0, 0)),
            pl.BlockSpec((1, D_MODEL, D_MODEL), lambda i, n: (i, 0, n)),
            pl.BlockSpec((1, 1, D_MODEL), lambda i, n: (i, 0, n)),
        ],
        out_specs=pl.BlockSpec((1, rows, D_MODEL), lambda i, n: (i, 0, n)),
        compiler_params=_cparams(("arbitrary", "arbitrary")),
        name="ada_mod",
    )(c_pad, ada_w, ada_b.reshape(DEPTH, 1, 3 * D_MODEL))


def _modulated_norm(x_ref, mod_ref, g_ref):
    x = x_ref[0]
    shift = mod_ref[0, :, 0:D_MODEL]
    scale = mod_ref[0, :, D_MODEL:2 * D_MODEL]
    return _rms(x, g_ref[...]) * (1.0 + scale) + shift


def _pipelined_key_loop(scores_fn, values_fn, s_ref, m_ref, acc_ref, *, steps_per_trip,
                        first_trip, last_trip):
    def step(t_cons, slot, col_max, t_prod):
        thunks = scores_fn(t_prod) if t_prod is not None else None
        next_max = []
        for c, vt in enumerate(values_fn(t_cons)):
            if thunks is not None:
                s = thunks[c]()
                s_ref[1 - slot, c] = s
                next_max.append(jnp.max(s, axis=0, keepdims=True))
            m_prev = m_ref[c]
            m_new = jnp.maximum(m_prev, col_max[c])
            p = jnp.exp2(s_ref[slot, c] - m_new).astype(BF16)
            alpha = jnp.exp2(m_prev - m_new)
            m_ref[c] = m_new
            acc_ref[c] = alpha * acc_ref[c] + _dot(vt, p)
        return tuple(next_max)

    def trip(i, col_max, last):
        t0 = steps_per_trip * i
        for u in range(steps_per_trip):
            final_step = last and u == steps_per_trip - 1
            col_max = step(t0 + u, u % 2, col_max, None if final_step else t0 + u + 1)
        return col_max

    assert steps_per_trip % 2 == 0
    cm = []
    for c, thunk in enumerate(scores_fn(steps_per_trip * first_trip)):
        s = thunk()
        s_ref[0, c] = s
        cm.append(jnp.max(s, axis=0, keepdims=True))
    cm = lax.fori_loop(first_trip, last_trip, lambda i, cm: trip(i, cm, False), tuple(cm))
    trip(last_trip, cm, True)


def _mla_pre_kernel(x_ref, mod_ref, g_ref, pos_ref, invf_ref, win_ref, qg_ref, wq_ref,
                    kvg_ref, wkv_ref, qt_out, k_out, vt_out, sg_out):
    h = _modulated_norm(x_ref, mod_ref, g_ref).astype(BF16)
    proj = _dot(h, win_ref[...])
    o1 = MLA_Q_RANK
    o2 = o1 + MLA_KV_RANK
    o3 = o2 + HEAD_PAD
    o4 = o3 + HEAD_PAD
    qn = _rms(proj[:, :o1], qg_ref[...]).astype(BF16)
    kvn = _rms(proj[:, o1:o2], kvg_ref[...]).astype(BF16)
    ang = pos_ref[0].astype(F32) * invf_ref[...]
    cos = jnp.cos(ang)
    sin = jnp.sin(ang)
    k_rope = proj[:, o2:o3] * cos + proj[:, o3:o4] * sin
    sg_out[0] = _silu(proj[:, o4:])

    qq = _dot(qn, wq_ref[...])
    kv = _dot(kvn, wkv_ref[...])
    half = MLA_HEADS * HEAD_PAD
    qscale = (MLA_QK ** -0.5) * LOG2E
    lane = lax.broadcasted_iota(jnp.int32, (1, HEAD_PAD), 1)
    ones_col = (lane == MLA_V).astype(F32)
    for hd in range(MLA_HEADS):
        a = hd * HEAD_PAD
        b = a + HEAD_PAD
        q_h = (qq[:, a:b] * cos + qq[:, half + a:half + b] * sin) * qscale
        qt_out[0, hd] = q_h.T.astype(BF16)
        k_out[0, hd] = (kv[:, a:b] + k_rope).astype(BF16)
        v_h = kv[:, half + a:half + b] + ones_col
        vt_out[0, hd] = v_h.T[:MLA_VT_ROWS].astype(BF16)


def _mla_pre(x, mod_i, g, pos_col, invf, win, qg, wq, kvg, wkv):
    B, S, _ = x.shape
    T = ROW_TILE
    const = lambda shape: pl.BlockSpec(shape, lambda b, t: (0,) * len(shape))
    return pl.pallas_call(
        _mla_pre_kernel,
        out_shape=(
            jax.ShapeDtypeStruct((B, MLA_HEADS, HEAD_PAD, S), BF16),
            jax.ShapeDtypeStruct((B, MLA_HEADS, S, HEAD_PAD), BF16),
            jax.ShapeDtypeStruct((B, MLA_HEADS, MLA_VT_ROWS, S), BF16),
            jax.ShapeDtypeStruct((B, S, D_MODEL), F32),
        ),
        grid=(B, S // T),
        in_specs=[
            pl.BlockSpec((1, T, D_MODEL), lambda b, t: (b, t, 0)),
            pl.BlockSpec((1, 1, 3 * D_MODEL), lambda b, t: (b, 0, 0)),
            const((1, D_MODEL)),
            pl.BlockSpec((1, T, 1), lambda b, t: (b, t, 0)),
            const((1, HEAD_PAD)),
            const(win.shape),
            const((1, MLA_Q_RANK)),
            const(wq.shape),
            const((1, MLA_KV_RANK)),
            const(wkv.shape),
        ],
        out_specs=(
            pl.BlockSpec((1, MLA_HEADS, HEAD_PAD, T), lambda b, t: (b, 0, 0, t)),
            pl.BlockSpec((1, MLA_HEADS, T, HEAD_PAD), lambda b, t: (b, 0, t, 0)),
            pl.BlockSpec((1, MLA_HEADS, MLA_VT_ROWS, T), lambda b, t: (b, 0, 0, t)),
            pl.BlockSpec((1, T, D_MODEL), lambda b, t: (b, t, 0)),
        ),
        compiler_params=_cparams(("parallel", "parallel")),
        name="mla_pre",
    )(x, mod_i, g, pos_col, invf, win, qg, wq, kvg, wkv)


MLA_HEADS_PER_STEP = 4


def _key_slice(t, tile):
    return pl.ds(pl.multiple_of(t * tile, tile), tile)


def _key_scores(k_ref, qt_ref, hh, t):
    return _dot(k_ref[0, hh, _key_slice(t, MLA_K_TILE), :], qt_ref[0, hh])


def _mla_attn_kernel(qt_ref, k_ref, vt_ref, sg_ref, o_ref, s_ref, m_ref, acc_ref):
    n_kv = k_ref.shape[2] // MLA_K_TILE
    m_ref[...] = jnp.full(m_ref.shape, -jnp.inf, F32)
    acc_ref[...] = jnp.zeros(acc_ref.shape, F32)

    heads = range(MLA_HEADS_PER_STEP)

    def scores_fn(t):
        return [functools.partial(_key_scores, k_ref, qt_ref, hh, t) for hh in heads]

    def values_fn(t):
        return [vt_ref[0, hh, :, _key_slice(t, MLA_K_TILE)] for hh in heads]

    assert n_kv % MLA_STEPS_PER_TRIP == 0
    _pipelined_key_loop(scores_fn, values_fn, s_ref, m_ref, acc_ref,
                        steps_per_trip=MLA_STEPS_PER_TRIP, first_trip=0,
                        last_trip=n_kv // MLA_STEPS_PER_TRIP - 1)

    outs = []
    for hh in range(MLA_HEADS_PER_STEP):
        acc = acc_ref[hh]
        outs.append(acc[:MLA_V] / acc[MLA_V:MLA_V + 1])
    o = jnp.concatenate(outs, axis=0).T
    o_ref[0] = (o * sg_ref[0]).astype(BF16)


def _mla_attn(qt, k, vt, sg):
    B, H, S, _ = k.shape
    G = MLA_HEADS_PER_STEP
    return pl.pallas_call(
        _mla_attn_kernel,
        out_shape=jax.ShapeDtypeStruct((B, S, H * MLA_V), BF16),
        grid=(B, H // G, S // MLA_Q_TILE),
        in_specs=[
            pl.BlockSpec((1, G, HEAD_PAD, MLA_Q_TILE), lambda b, h, i: (b, h, 0, i)),
            pl.BlockSpec((1, G, S, HEAD_PAD), lambda b, h, i: (b, h, 0, 0)),
            pl.BlockSpec((1, G, MLA_VT_ROWS, S), lambda b, h, i: (b, h, 0, 0)),
            pl.BlockSpec((1, MLA_Q_TILE, G * MLA_V), lambda b, h, i: (b, i, h)),
        ],
        out_specs=pl.BlockSpec((1, MLA_Q_TILE, G * MLA_V), lambda b, h, i: (b, i, h)),
        scratch_shapes=[
            pltpu.VMEM((2, G, MLA_K_TILE, MLA_Q_TILE), F32),
            pltpu.VMEM((G, 1, MLA_Q_TILE), F32),
            pltpu.VMEM((G, MLA_VT_ROWS, MLA_Q_TILE), F32),
        ],
        compiler_params=_cparams(("parallel", "parallel", "parallel")),
        name="mla_attn",
    )(qt, k, vt, sg)


def _diff_pre_kernel(x_ref, mod_ref, g_ref, win_ref, qt_out, k_out, vt_out, sg_out, kn_out):
    h = _modulated_norm(x_ref, mod_ref, g_ref).astype(BF16)
    proj = _dot(h, win_ref[...])
    qscale = (DIFF_HD ** -0.5) * LOG2E
    sg_out[0] = _silu(proj[:, 3 * DIFF_W:])
    T = proj.shape[0]
    sub = lax.broadcasted_iota(jnp.int32, (BF16_SUBLANES, T), 0)
    ones_rows = (sub == 0).astype(F32)
    k_sq_max = []
    for hd in range(DIFF_HEADS):
        a = hd * HEAD_PAD
        b = a + HEAD_PAD
        qt_out[0, hd] = (proj[:, a:b] * qscale).T.astype(BF16)
        k_h = proj[:, DIFF_W + a:DIFF_W + b]
        k_out[0, hd] = k_h.astype(BF16)
        k_sq = jnp.sum(k_h * k_h, axis=-1, keepdims=True)
        k_sq_max.append(jnp.broadcast_to(jnp.max(k_sq, axis=0, keepdims=True), (1, LANES)))
        v_t = proj[:, 2 * DIFF_W + a:2 * DIFF_W + b].T
        vt_out[0, hd] = jnp.concatenate([v_t, ones_rows], axis=0).astype(BF16)
    kn_out[0, 0] = jnp.concatenate(k_sq_max, axis=0)


def _diff_pre(x, mod_i, g, win):
    B, S, _ = x.shape
    T = ROW_TILE
    const = lambda shape: pl.BlockSpec(shape, lambda b, t: (0,) * len(shape))
    return pl.pallas_call(
        _diff_pre_kernel,
        out_shape=(
            jax.ShapeDtypeStruct((B, DIFF_HEADS, HEAD_PAD, S), BF16),
            jax.ShapeDtypeStruct((B, DIFF_HEADS, S, HEAD_PAD), BF16),
            jax.ShapeDtypeStruct((B, DIFF_HEADS, DIFF_VT_ROWS, S), BF16),
            jax.ShapeDtypeStruct((B, S, D_MODEL), F32),
            jax.ShapeDtypeStruct((B, S // T, DIFF_HEADS, LANES), F32),
        ),
        grid=(B, S // T),
        in_specs=[
            pl.BlockSpec((1, T, D_MODEL), lambda b, t: (b, t, 0)),
            pl.BlockSpec((1, 1, 3 * D_MODEL), lambda b, t: (b, 0, 0)),
            const((1, D_MODEL)),
            const(win.shape),
        ],
        out_specs=(
            pl.BlockSpec((1, DIFF_HEADS, HEAD_PAD, T), lambda b, t: (b, 0, 0, t)),
            pl.BlockSpec((1, DIFF_HEADS, T, HEAD_PAD), lambda b, t: (b, 0, t, 0)),
            pl.BlockSpec((1, DIFF_HEADS, DIFF_VT_ROWS, T), lambda b, t: (b, 0, 0, t)),
            pl.BlockSpec((1, T, D_MODEL), lambda b, t: (b, t, 0)),
            pl.BlockSpec((1, 1, DIFF_HEADS, LANES), lambda b, t: (b, t, 0, 0)),
        ),
        compiler_params=_cparams(("parallel", "parallel")),
        name="diff_pre",
    )(x, mod_i, g, win)


def _needed_trips(qt, posq, posrow_ref, kn_ref, slope, n_trips, keys_per_trip):
    qf = qt.astype(F32)
    q_sq = jnp.max(jnp.sum(qf * qf, axis=0, keepdims=True))
    score_bound = jnp.sqrt(q_sq * jnp.max(kn_ref[0])) * NORM_MARGIN
    limit = PRUNE_LOG2 + 2.0 * score_bound
    q_lo = jnp.min(posq)
    q_hi = jnp.max(posq)
    first = jnp.int32(n_trips - 1)
    last = jnp.int32(0)
    for p in range(n_trips):
        pk = posrow_ref[0, :, p * keys_per_trip:(p + 1) * keys_per_trip]
        gap = jnp.maximum(jnp.maximum(jnp.min(pk) - q_hi, q_lo - jnp.max(pk)), 0)
        needed = gap.astype(F32) * slope <= limit
        first = jnp.where(needed, jnp.minimum(first, p), first)
        last = jnp.where(needed, jnp.maximum(last, p), last)
    return first, jnp.maximum(last, first)


def _diff_attn_kernel(qt_ref, k_ref, vt_ref, sg_ref, posq_ref, posk_ref, posrow_ref, kn_ref,
                      slope_ref, lq1_ref, lk1_ref, lq2_ref, lk2_ref, hg_ref, o_ref,
                      s_ref, m_ref, acc_ref, *, lambda_init):
    n_kv = k_ref.shape[2] // DIFF_K_TILE
    m_ref[...] = jnp.full(m_ref.shape, -jnp.inf, F32)
    acc_ref[...] = jnp.zeros(acc_ref.shape, F32)

    qt = qt_ref[0, 0]
    row = lax.broadcasted_iota(jnp.int32, qt.shape, 0)
    zero = jnp.zeros_like(qt)
    qt_maps = (jnp.where(row < DIFF_HD, qt, zero), jnp.where(row >= DIFF_HD, qt, zero))
    slope = slope_ref[0][:, 0:1]
    pos_q = posq_ref[0].astype(F32) * slope

    def scores_fn(t):
        keys = _key_slice(t, DIFF_K_TILE)
        pos_k = posk_ref[0, keys, :].astype(F32) * slope
        bias = jnp.abs(pos_k - pos_q)
        kk = k_ref[0, 0, keys, :]
        return [lambda qt_c=qt_c: _dot(kk, qt_c) - bias for qt_c in qt_maps]

    def values_fn(t):
        vt = vt_ref[0, 0, :, _key_slice(t, DIFF_K_TILE)]
        return [vt, vt]

    assert n_kv % DIFF_STEPS_PER_TRIP == 0
    first_trip, last_trip = _needed_trips(
        qt, posq_ref[0], posrow_ref, kn_ref, jnp.max(slope_ref[0]),
        n_kv // DIFF_STEPS_PER_TRIP, DIFF_STEPS_PER_TRIP * DIFF_K_TILE)
    _pipelined_key_loop(scores_fn, values_fn, s_ref, m_ref, acc_ref,
                        steps_per_trip=DIFF_STEPS_PER_TRIP, first_trip=first_trip,
                        last_trip=last_trip)

    lam =(jnp.exp(jnp.sum(lq1_ref[...] * lk1_ref[...], axis=-1, keepdims=True))
           - jnp.exp(jnp.sum(lq2_ref[...] * lk2_ref[...], axis=-1, keepdims=True))
           + lambda_init)
    a1 = acc_ref[0]
    a2 = acc_ref[1]
    o = a1[:DIFF_V] / a1[DIFF_V:DIFF_V + 1] - lam * (a2[:DIFF_V] / a2[DIFF_V:DIFF_V + 1])
    o = o * lax.rsqrt(jnp.mean(o * o, axis=0, keepdims=True) + EPS)
    o = o.T * (hg_ref[...] * (1.0 - lambda_init))
    o_ref[0] = (o * sg_ref[0]).astype(BF16)


def _diff_attn(qt, k, vt, sg, pos_col, pos_row, k_sq_max, slopes, lq1, lk1, lq2, lk2, hg, lambda_init):
    B, H, S, _ = k.shape
    vec = lambda n: pl.BlockSpec((1, n), lambda b, h, i: (0, 0))
    return pl.pallas_call(
        functools.partial(_diff_attn_kernel, lambda_init=lambda_init),
        out_shape=jax.ShapeDtypeStruct((B, S, DIFF_W), BF16),
        grid=(B, H, S // DIFF_Q_TILE),
        in_specs=[
            pl.BlockSpec((1, 1, HEAD_PAD, DIFF_Q_TILE), lambda b, h, i: (b, h, 0, i)),
            pl.BlockSpec((1, 1, S, HEAD_PAD), lambda b, h, i: (b, h, 0, 0)),
            pl.BlockSpec((1, 1, DIFF_VT_ROWS, S), lambda b, h, i: (b, h, 0, 0)),
            pl.BlockSpec((1, DIFF_Q_TILE, HEAD_PAD), lambda b, h, i: (b, i, h)),
            pl.BlockSpec((1, 1, DIFF_Q_TILE), lambda b, h, i: (b, 0, i)),
            pl.BlockSpec((1, S, 1), lambda b, h, i: (b, 0, 0)),
            pl.BlockSpec((1, 1, S), lambda b, h, i: (b, 0, 0)),
            pl.BlockSpec((1,) + k_sq_max.shape[1:], lambda b, h, i: (b, 0, 0, 0)),
            pl.BlockSpec((1, 1, LANES), lambda b, h, i: (h, 0, 0)),
            vec(DIFF_HD), vec(DIFF_HD), vec(DIFF_HD), vec(DIFF_HD),
            vec(DIFF_V),
        ],
        out_specs=pl.BlockSpec((1, DIFF_Q_TILE, HEAD_PAD), lambda b, h, i: (b, i, h)),
        scratch_shapes=[
            pltpu.VMEM((2, 2, DIFF_K_TILE, DIFF_Q_TILE), F32),
            pltpu.VMEM((2, 1, DIFF_Q_TILE), F32),
            pltpu.VMEM((2, DIFF_VT_ROWS, DIFF_Q_TILE), F32),
        ],
        compiler_params=_cparams(("parallel", "parallel", "parallel")),
        name="diff_attn",
    )(qt, k, vt, sg, pos_row, pos_col, pos_row, k_sq_max, slopes, lq1, lk1, lq2, lk2, hg)


def _post_kernel(o_ref, x_ref, mod_ref, wo_ref, fg_ref, out_ref, *, final):
    y = _dot(o_ref[0], wo_ref[...])
    gate = mod_ref[0, :, 2 * D_MODEL:]
    x_new = x_ref[0] + gate * y
    if final:
        x_new = _rms(x_new, fg_ref[...])
    out_ref[0] = x_new


def _post(o, x, mod_i, wo, fg, final):
    B, S, _ = x.shape
    T = ROW_TILE
    const = lambda shape: pl.BlockSpec(shape, lambda b, t: (0,) * len(shape))
    return pl.pallas_call(
        functools.partial(_post_kernel, final=final),
        out_shape=jax.ShapeDtypeStruct((B, S, D_MODEL), F32),
        grid=(B, S // T),
        in_specs=[
            pl.BlockSpec((1, T, D_MODEL), lambda b, t: (b, t, 0)),
            pl.BlockSpec((1, T, D_MODEL), lambda b, t: (b, t, 0)),
            pl.BlockSpec((1, 1, 3 * D_MODEL), lambda b, t: (b, 0, 0)),
            const(wo.shape),
            const((1, D_MODEL)),
        ],
        out_specs=pl.BlockSpec((1, T, D_MODEL), lambda b, t: (b, t, 0)),
        compiler_params=_cparams(("parallel", "parallel")),
        name="post_final" if final else "post",
    )(o, x, mod_i, wo, fg)


def _rot_cols(w):
    half = MLA_ROPE // 2
    return jnp.concatenate([-w[..., half:], w[..., :half]], axis=-1)


def _pad_slab(w, lo):
    pad = [(0, 0)] * (w.ndim - 1) + [(lo, HEAD_PAD - lo - w.shape[-1])]
    return jnp.pad(w, pad)


def _mla_weights(w_in, w_q_up, w_kv_up):
    o1 = MLA_Q_RANK
    o2 = o1 + MLA_KV_RANK
    o3 = o2 + MLA_ROPE
    w_kr = w_in[:, o2:o3]
    win = jnp.concatenate([
        w_in[:, :o2],
        _pad_slab(w_kr, MLA_NOPE),
        _pad_slab(_rot_cols(w_kr), MLA_NOPE),
        w_in[:, o3:],
    ], axis=1).astype(BF16)
    wq3 = w_q_up.reshape(MLA_Q_RANK, MLA_HEADS, MLA_QK)
    wq_plain = _pad_slab(wq3, 0).reshape(MLA_Q_RANK, MLA_HEADS * HEAD_PAD)
    wq_rot = _pad_slab(_rot_cols(wq3[..., MLA_NOPE:]), MLA_NOPE).reshape(MLA_Q_RANK, MLA_HEADS * HEAD_PAD)
    wq = jnp.concatenate([wq_plain, wq_rot], axis=1).astype(BF16)
    wkv3 = w_kv_up.reshape(MLA_KV_RANK, MLA_HEADS, MLA_NOPE + MLA_V)
    wk = _pad_slab(wkv3[..., :MLA_NOPE], 0).reshape(MLA_KV_RANK, MLA_HEADS * HEAD_PAD)
    wv = _pad_slab(wkv3[..., MLA_NOPE:], 0).reshape(MLA_KV_RANK, MLA_HEADS * HEAD_PAD)
    wkv = jnp.concatenate([wk, wv], axis=1).astype(BF16)
    return win, wq, wkv


def kernel(x, c, positions, ada_w, ada_b, norm_g, mla_w_in, mla_q_norm_g, mla_w_q_up, mla_kv_norm_g,
           mla_w_kv_up, mla_w_o, diff_w_in, diff_lq1, diff_lk1, diff_lq2, diff_lk2, diff_head_g,
           diff_w_o, final_g):
    B, S, D = x.shape
    assert D == D_MODEL and S % ROW_TILE == 0
    assert S % MLA_Q_TILE == 0 and S % MLA_K_TILE == 0 and S % DIFF_Q_TILE == 0 and S % DIFF_K_TILE == 0
    c_pad = jnp.pad(c, ((0, 8 - B), (0, 0)))
    mod = _ada_mod(c_pad, ada_w, ada_b)[:, :B].reshape(DEPTH, B, 1, 3 * D)

    pos_col = positions.reshape(B, S, 1)
    pos_row = positions.reshape(B, 1, S)
    inv = ROPE_BASE ** (-jnp.arange(0, MLA_ROPE, 2, dtype=F32) / MLA_ROPE)
    invf = _pad_slab(jnp.concatenate([inv, inv]), MLA_NOPE).reshape(1, HEAD_PAD)
    slopes = jnp.exp2(-8.0 * jnp.arange(1, DIFF_HEADS + 1, dtype=F32) / DIFF_HEADS) * LOG2E
    slopes = jnp.broadcast_to(slopes[:, None, None], (DIFF_HEADS, 1, LANES))
    final_row = final_g.reshape(1, D)

    for i in range(DEPTH):
        j = i // 2
        g = norm_g[i].reshape(1, D)
        last = i == DEPTH - 1
        if i % 2 == 0:
            win, wq, wkv = _mla_weights(mla_w_in[j], mla_w_q_up[j], mla_w_kv_up[j])
            qt, k, vt, sg = _mla_pre(x, mod[i], g, pos_col, invf, win,
                                     mla_q_norm_g[j].reshape(1, -1), wq,
                                     mla_kv_norm_g[j].reshape(1, -1), wkv)
            o = _mla_attn(qt, k, vt, sg)
            wo = mla_w_o[j].astype(BF16)
        else:
            lambda_init = 0.8 - 0.6 * math.exp(-0.3 * i)
            qt, k, vt, sg, k_sq_max = _diff_pre(x, mod[i], g, diff_w_in[j].astype(BF16))
            o = _diff_attn(qt, k, vt, sg, pos_col, pos_row, k_sq_max, slopes,
                           diff_lq1[j].reshape(1, -1), diff_lk1[j].reshape(1, -1),
                           diff_lq2[j].reshape(1, -1), diff_lk2[j].reshape(1, -1),
                           diff_head_g[j].reshape(1, -1), lambda_init)
            wo = diff_w_o[j].astype(BF16)
        x = _post(o, x, mod[i], wo, final_row, last)
    return x
```

```python
import functools
import math

import jax
import jax.numpy as jnp
from jax import lax
from jax.experimental import pallas as pl
from jax.experimental.pallas import tpu as pltpu

D_MODEL = 1024
DEPTH = 4
EPS = 1e-6

MLA_HEADS = 16
MLA_NOPE = 64
MLA_ROPE = 32
MLA_V = 64
MLA_Q_RANK = 384
MLA_KV_RANK = 256
ROPE_BASE = 10000.0
MLA_QK = MLA_NOPE + MLA_ROPE

DIFF_HD = 64
DIFF_HEADS = D_MODEL // (2 * DIFF_HD)
DIFF_W = DIFF_HEADS * 2 * DIFF_HD
DIFF_V = 2 * DIFF_HD

LANES = 128
BF16_SUBLANES = 16
HEAD_PAD = LANES
LOG2E = math.log2(math.e)

MLA_VT_ROWS = MLA_V + BF16_SUBLANES
DIFF_VT_ROWS = DIFF_V + BF16_SUBLANES

ROW_TILE = 256
DIFF_PRE_ROW_TILE = 512
POST_ROW_TILE = 1024
ROPE_ROW_TILE = 512
MLA_Q_TILE = 512
MLA_K_TILE = 512
DIFF_Q_TILE = 512
DIFF_K_TILE = 256
MLA_STEPS_PER_TRIP = 4
DIFF_STEPS_PER_TRIP = 2
PRUNE_LOG2 = 150.0
NORM_MARGIN = 1.02
VMEM_LIMIT = 56 * 1024 * 1024

BF16 = jnp.bfloat16
F32 = jnp.float32


def _cparams(sem):
    return pltpu.CompilerParams(dimension_semantics=sem, vmem_limit_bytes=VMEM_LIMIT)


def _dot(a, b):
    return jnp.dot(a, b, preferred_element_type=F32)


def _rms(x, g):
    return x * lax.rsqrt(jnp.mean(x * x, axis=-1, keepdims=True) + EPS) * g


def _silu(x):
    return x * (1.0 / (1.0 + jnp.exp(-x)))


def _mod_kernel(c_ref, w_ref, b_ref, o_ref):
    c = c_ref[...]
    o_ref[0] = _dot(_silu(c).astype(BF16), w_ref[0].astype(BF16)) + b_ref[0]


def _ada_mod(c_pad, ada_w, ada_b):
    rows = c_pad.shape[0]
    nblk = 3
    return pl.pallas_call(
        _mod_kernel,
        out_shape=jax.ShapeDtypeStruct((DEPTH, rows, 3 * D_MODEL), F32),
        grid=(DEPTH, nblk),
        in_specs=[
            pl.BlockSpec((rows, D_MODEL), lambda i, n: (0, 0)),
            pl.BlockSpec((1, D_MODEL, D_MODEL), lambda i, n: (i, 0, n)),
            pl.BlockSpec((1, 1, D_MODEL), lambda i, n: (i, 0, n)),
        ],
        out_specs=pl.BlockSpec((1, rows, D_MODEL), lambda i, n: (i, 0, n)),
        compiler_params=_cparams(("arbitrary", "arbitrary")),
        name="ada_mod",
    )(c_pad, ada_w, ada_b.reshape(DEPTH, 1, 3 * D_MODEL))


def _modulated_norm(x_ref, mod_ref, g_ref):
    x = x_ref[0]
    shift = mod_ref[0, :, 0:D_MODEL]
    scale = mod_ref[0, :, D_MODEL:2 * D_MODEL]
    return _rms(x, g_ref[...]) * (1.0 + scale) + shift


def _pipelined_key_loop(scores_fn, values_fn, s_ref, m_ref, acc_ref, *, steps_per_trip,
                        first_trip, last_trip):
    def step(t_cons, slot, col_max, t_prod):
        thunks = scores_fn(t_prod) if t_prod is not None else None
        next_max = []
        for c, vt in enumerate(values_fn(t_cons)):
            if thunks is not None:
                s = thunks[c]()
                s_ref[1 - slot, c] = s
                next_max.append(jnp.max(s, axis=0, keepdims=True))
            m_prev = m_ref[c]
            m_new = jnp.maximum(m_prev, col_max[c])
            p = jnp.exp2(s_ref[slot, c] - m_new).astype(BF16)
            alpha = jnp.exp2(m_prev - m_new)
            m_ref[c] = m_new
            acc_ref[c] = alpha * acc_ref[c] + _dot(vt, p)
        return tuple(next_max)

    def trip(i, col_max, last):
        t0 = steps_per_trip * i
        for u in range(steps_per_trip):
            final_step = last and u == steps_per_trip - 1
            col_max = step(t0 + u, u % 2, col_max, None if final_step else t0 + u + 1)
        return col_max

    assert steps_per_trip % 2 == 0
    cm = []
    for c, thunk in enumerate(scores_fn(steps_per_trip * first_trip)):
        s = thunk()
        s_ref[0, c] = s
        cm.append(jnp.max(s, axis=0, keepdims=True))
    cm = lax.fori_loop(first_trip, last_trip, lambda i, cm: trip(i, cm, False), tuple(cm))
    trip(last_trip, cm, True)


def _rope_kernel(pos_ref, inv_ref, cos_out, sin_out):
    ang = inv_ref[...] * pos_ref[0].astype(F32)
    T = ang.shape[1]
    ones = jnp.ones((MLA_NOPE, T), F32)
    tail = HEAD_PAD - MLA_QK
    cos_out[0] = jnp.concatenate([ones, jnp.cos(ang), ones[:tail]], axis=0).T
    sin_out[0] = jnp.concatenate([0.0 * ones, jnp.sin(ang), 0.0 * ones[:tail]], axis=0).T


def _rope_tables(pos_row, inv_col):
    B, _, S = pos_row.shape
    T = ROPE_ROW_TILE
    table = jax.ShapeDtypeStruct((B, S, HEAD_PAD), F32)
    return pl.pallas_call(
        _rope_kernel,
        out_shape=(table, table),
        grid=(B, S // T),
        in_specs=[
            pl.BlockSpec((1, 1, T), lambda b, t: (b, 0, t)),
            pl.BlockSpec((MLA_ROPE, 1), lambda b, t: (0, 0)),
        ],
        out_specs=(
            pl.BlockSpec((1, T, HEAD_PAD), lambda b, t: (b, t, 0)),
            pl.BlockSpec((1, T, HEAD_PAD), lambda b, t: (b, t, 0)),
        ),
        compiler_params=_cparams(("parallel", "parallel")),
        name="rope_tables",
    )(pos_row, inv_col)


def _mla_pre_kernel(x_ref, mod_ref, g_ref, cos_ref, sin_ref, win_ref, qg_ref, wq_ref,
                    kvg_ref, wkv_ref, qt_out, k_out, vt_out, sg_out):
    h = _modulated_norm(x_ref, mod_ref, g_ref).astype(BF16)
    proj = _dot(h, win_ref[...])
    o1 = MLA_Q_RANK
    o2 = o1 + MLA_KV_RANK
    o3 = o2 + HEAD_PAD
    o4 = o3 + HEAD_PAD
    qn = _rms(proj[:, :o1], qg_ref[...]).astype(BF16)
    kvn = _rms(proj[:, o1:o2], kvg_ref[...]).astype(BF16)
    cos = cos_ref[0]
    sin = sin_ref[0]
    k_rope = proj[:, o2:o3] * cos + proj[:, o3:o4] * sin
    sg_out[0] = _silu(proj[:, o4:])

    qq = _dot(qn, wq_ref[...])
    kv = _dot(kvn, wkv_ref[...])
    half = MLA_HEADS * HEAD_PAD
    qscale = (MLA_QK ** -0.5) * LOG2E
    lane = lax.broadcasted_iota(jnp.int32, (1, HEAD_PAD), 1)
    ones_col = (lane == MLA_V).astype(F32)
    for hd in range(MLA_HEADS):
        a = hd * HEAD_PAD
        b = a + HEAD_PAD
        q_h = (qq[:, a:b] * cos + qq[:, half + a:half + b] * sin) * qscale
        qt_out[0, hd] = q_h.T.astype(BF16)
        k_out[0, hd] = (kv[:, a:b] + k_rope).astype(BF16)
        v_h = kv[:, half + a:half + b] + ones_col
        vt_out[0, hd] = v_h.T[:MLA_VT_ROWS].astype(BF16)


def _mla_pre(x, mod_i, g, cos, sin, win, qg, wq, kvg, wkv):
    B, S, _ = x.shape
    T = ROW_TILE
    const = lambda shape: pl.BlockSpec(shape, lambda b, t: (0,) * len(shape))
    return pl.pallas_call(
        _mla_pre_kernel,
        out_shape=(
            jax.ShapeDtypeStruct((B, MLA_HEADS, HEAD_PAD, S), BF16),
            jax.ShapeDtypeStruct((B, MLA_HEADS, S, HEAD_PAD), BF16),
            jax.ShapeDtypeStruct((B, MLA_HEADS, MLA_VT_ROWS, S), BF16),
            jax.ShapeDtypeStruct((B, S, D_MODEL), F32),
        ),
        grid=(B, S // T),
        in_specs=[
            pl.BlockSpec((1, T, D_MODEL), lambda b, t: (b, t, 0)),
            pl.BlockSpec((1, 1, 3 * D_MODEL), lambda b, t: (b, 0, 0)),
            const((1, D_MODEL)),
            pl.BlockSpec((1, T, HEAD_PAD), lambda b, t: (b, t, 0)),
            pl.BlockSpec((1, T, HEAD_PAD), lambda b, t: (b, t, 0)),
            const(win.shape),
            const((1, MLA_Q_RANK)),
            const(wq.shape),
            const((1, MLA_KV_RANK)),
            const(wkv.shape),
        ],
        out_specs=(
            pl.BlockSpec((1, MLA_HEADS, HEAD_PAD, T), lambda b, t: (b, 0, 0, t)),
            pl.BlockSpec((1, MLA_HEADS, T, HEAD_PAD), lambda b, t: (b, 0, t, 0)),
            pl.BlockSpec((1, MLA_HEADS, MLA_VT_ROWS, T), lambda b, t: (b, 0, 0, t)),
            pl.BlockSpec((1, T, D_MODEL), lambda b, t: (b, t, 0)),
        ),
        compiler_params=_cparams(("parallel", "parallel")),
        name="mla_pre",
    )(x, mod_i, g, cos, sin, win, qg, wq, kvg, wkv)


MLA_HEADS_PER_STEP = 4


def _key_slice(t, tile):
    return pl.ds(pl.multiple_of(t * tile, tile), tile)


def _key_scores(k_ref, qt_ref, hh, t):
    return _dot(k_ref[0, hh, _key_slice(t, MLA_K_TILE), :], qt_ref[0, hh])


def _mla_attn_kernel(qt_ref, k_ref, vt_ref, sg_ref, o_ref, s_ref, m_ref, acc_ref):
    n_kv = k_ref.shape[2] // MLA_K_TILE
    m_ref[...] = jnp.full(m_ref.shape, -jnp.inf, F32)
    acc_ref[...] = jnp.zeros(acc_ref.shape, F32)

    heads = range(MLA_HEADS_PER_STEP)

    def scores_fn(t):
        return [functools.partial(_key_scores, k_ref, qt_ref, hh, t) for hh in heads]

    def values_fn(t):
        return [vt_ref[0, hh, :, _key_slice(t, MLA_K_TILE)] for hh in heads]

    assert n_kv % MLA_STEPS_PER_TRIP == 0
    _pipelined_key_loop(scores_fn, values_fn, s_ref, m_ref, acc_ref,
                        steps_per_trip=MLA_STEPS_PER_TRIP, first_trip=0,
                        last_trip=n_kv // MLA_STEPS_PER_TRIP - 1)

    outs = []
    for hh in range(MLA_HEADS_PER_STEP):
        acc = acc_ref[hh]
        outs.append(acc[:MLA_V] / acc[MLA_V:MLA_V + 1])
    o = jnp.concatenate(outs, axis=0).T
    o_ref[0] = (o * sg_ref[0]).astype(BF16)


def _mla_attn(qt, k, vt, sg):
    B, H, S, _ = k.shape
    G = MLA_HEADS_PER_STEP
    return pl.pallas_call(
        _mla_attn_kernel,
        out_shape=jax.ShapeDtypeStruct((B, S, H * MLA_V), BF16),
        grid=(B, H // G, S // MLA_Q_TILE),
        in_specs=[
            pl.BlockSpec((1, G, HEAD_PAD, MLA_Q_TILE), lambda b, h, i: (b, h, 0, i)),
            pl.BlockSpec((1, G, S, HEAD_PAD), lambda b, h, i: (b, h, 0, 0)),
            pl.BlockSpec((1, G, MLA_VT_ROWS, S), lambda b, h, i: (b, h, 0, 0)),
            pl.BlockSpec((1, MLA_Q_TILE, G * MLA_V), lambda b, h, i: (b, i, h)),
        ],
        out_specs=pl.BlockSpec((1, MLA_Q_TILE, G * MLA_V), lambda b, h, i: (b, i, h)),
        scratch_shapes=[
            pltpu.VMEM((2, G, MLA_K_TILE, MLA_Q_TILE), F32),
            pltpu.VMEM((G, 1, MLA_Q_TILE), F32),
            pltpu.VMEM((G, MLA_VT_ROWS, MLA_Q_TILE), F32),
        ],
        compiler_params=_cparams(("parallel", "parallel", "parallel")),
        name="mla_attn",
    )(qt, k, vt, sg)


def _diff_pre_kernel(x_ref, mod_ref, g_ref, win_ref, qt_out, k_out, vt_out, sg_out, kn_out):
    h = _modulated_norm(x_ref, mod_ref, g_ref).astype(BF16)
    proj = _dot(h, win_ref[...])
    qscale = (DIFF_HD ** -0.5) * LOG2E
    sg_out[0] = _silu(proj[:, 3 * DIFF_W:])
    T = proj.shape[0]
    sub = lax.broadcasted_iota(jnp.int32, (BF16_SUBLANES, T), 0)
    ones_rows = (sub == 0).astype(F32)
    k_sq_max = []
    for hd in range(DIFF_HEADS):
        a = hd * HEAD_PAD
        b = a + HEAD_PAD
        qt_out[0, hd] = (proj[:, a:b] * qscale).T.astype(BF16)
        k_h = proj[:, DIFF_W + a:DIFF_W + b]
        k_out[0, hd] = k_h.astype(BF16)
        k_sq = jnp.sum(k_h * k_h, axis=-1, keepdims=True)
        k_sq_max.append(jnp.broadcast_to(jnp.max(k_sq, axis=0, keepdims=True), (1, LANES)))
        v_t = proj[:, 2 * DIFF_W + a:2 * DIFF_W + b].T
        vt_out[0, hd] = jnp.concatenate([v_t, ones_rows], axis=0).astype(BF16)
    kn_out[0, 0] = jnp.concatenate(k_sq_max, axis=0)


def _diff_pre(x, mod_i, g, win):
    B, S, _ = x.shape
    T = DIFF_PRE_ROW_TILE
    const = lambda shape: pl.BlockSpec(shape, lambda b, t: (0,) * len(shape))
    return pl.pallas_call(
        _diff_pre_kernel,
        out_shape=(
            jax.ShapeDtypeStruct((B, DIFF_HEADS, HEAD_PAD, S), BF16),
            jax.ShapeDtypeStruct((B, DIFF_HEADS, S, HEAD_PAD), BF16),
            jax.ShapeDtypeStruct((B, DIFF_HEADS, DIFF_VT_ROWS, S), BF16),
            jax.ShapeDtypeStruct((B, S, D_MODEL), F32),
            jax.ShapeDtypeStruct((B, S // T, DIFF_HEADS, LANES), F32),
        ),
        grid=(B, S // T),
        in_specs=[
            pl.BlockSpec((1, T, D_MODEL), lambda b, t: (b, t, 0)),
            pl.BlockSpec((1, 1, 3 * D_MODEL), lambda b, t: (b, 0, 0)),
            const((1, D_MODEL)),
            const(win.shape),
        ],
        out_specs=(
            pl.BlockSpec((1, DIFF_HEADS, HEAD_PAD, T), lambda b, t: (b, 0, 0, t)),
            pl.BlockSpec((1, DIFF_HEADS, T, HEAD_PAD), lambda b, t: (b, 0, t, 0)),
            pl.BlockSpec((1, DIFF_HEADS, DIFF_VT_ROWS, T), lambda b, t: (b, 0, 0, t)),
            pl.BlockSpec((1, T, D_MODEL), lambda b, t: (b, t, 0)),
            pl.BlockSpec((1, 1, DIFF_HEADS, LANES), lambda b, t: (b, t, 0, 0)),
        ),
        compiler_params=_cparams(("parallel", "parallel")),
        name="diff_pre",
    )(x, mod_i, g, win)


def _needed_trips(qt, posq, posrow_ref, kn_ref, slope, n_trips, keys_per_trip):
    qf = qt.astype(F32)
    q_sq = jnp.max(jnp.sum(qf * qf, axis=0, keepdims=True))
    score_bound = jnp.sqrt(q_sq * jnp.max(kn_ref[0])) * NORM_MARGIN
    limit = PRUNE_LOG2 + 2.0 * score_bound
    q_lo = jnp.min(posq)
    q_hi = jnp.max(posq)
    first = jnp.int32(n_trips - 1)
    last = jnp.int32(0)
    for p in range(n_trips):
        pk = posrow_ref[0, :, p * keys_per_trip:(p + 1) * keys_per_trip]
        gap = jnp.maximum(jnp.maximum(jnp.min(pk) - q_hi, q_lo - jnp.max(pk)), 0)
        needed = gap.astype(F32) * slope <= limit
        first = jnp.where(needed, jnp.minimum(first, p), first)
        last = jnp.where(needed, jnp.maximum(last, p), last)
    return first, jnp.maximum(last, first)


def _diff_attn_kernel(qt_ref, k_ref, vt_ref, sg_ref, posq_ref, posk_ref, posrow_ref, kn_ref,
                      slope_ref, lq1_ref, lk1_ref, lq2_ref, lk2_ref, hg_ref, o_ref,
                      s_ref, m_ref, acc_ref, *, lambda_init):
    n_kv = k_ref.shape[2] // DIFF_K_TILE
    m_ref[...] = jnp.full(m_ref.shape, -jnp.inf, F32)
    acc_ref[...] = jnp.zeros(acc_ref.shape, F32)

    qt = qt_ref[0, 0]
    row = lax.broadcasted_iota(jnp.int32, qt.shape, 0)
    zero = jnp.zeros_like(qt)
    qt_maps = (jnp.where(row < DIFF_HD, qt, zero), jnp.where(row >= DIFF_HD, qt, zero))
    slope = slope_ref[0][:, 0:1]
    pos_q = posq_ref[0].astype(F32) * slope

    def scores_fn(t):
        keys = _key_slice(t, DIFF_K_TILE)
        pos_k = posk_ref[0, keys, :].astype(F32) * slope
        bias = jnp.abs(pos_k - pos_q)
        kk = k_ref[0, 0, keys, :]
        return [lambda qt_c=qt_c: _dot(kk, qt_c) - bias for qt_c in qt_maps]

    def values_fn(t):
        vt = vt_ref[0, 0, :, _key_slice(t, DIFF_K_TILE)]
        return [vt, vt]

    assert n_kv % DIFF_STEPS_PER_TRIP == 0
    first_trip, last_trip = _needed_trips(
        qt, posq_ref[0], posrow_ref, kn_ref, jnp.max(slope_ref[0]),
        n_kv // DIFF_STEPS_PER_TRIP, DIFF_STEPS_PER_TRIP * DIFF_K_TILE)
    _pipelined_key_loop(scores_fn, values_fn, s_ref, m_ref, acc_ref,
                        steps_per_trip=DIFF_STEPS_PER_TRIP, first_trip=first_trip,
                        last_trip=last_trip)

    lam =(jnp.exp(jnp.sum(lq1_ref[...] * lk1_ref[...], axis=-1, keepdims=True))
           - jnp.exp(jnp.sum(lq2_ref[...] * lk2_ref[...], axis=-1, keepdims=True))
           + lambda_init)
    a1 = acc_ref[0]
    a2 = acc_ref[1]
    o = a1[:DIFF_V] / a1[DIFF_V:DIFF_V + 1] - lam * (a2[:DIFF_V] / a2[DIFF_V:DIFF_V + 1])
    o = o * lax.rsqrt(jnp.mean(o * o, axis=0, keepdims=True) + EPS)
    o = o.T * (hg_ref[...] * (1.0 - lambda_init))
    o_ref[0] = (o * sg_ref[0]).astype(BF16)


def _diff_attn(qt, k, vt, sg, pos_col, pos_row, k_sq_max, slopes, lq1, lk1, lq2, lk2, hg, lambda_init):
    B, H, S, _ = k.shape
    vec = lambda n: pl.BlockSpec((1, n), lambda b, h, i: (0, 0))
    return pl.pallas_call(
        functools.partial(_diff_attn_kernel, lambda_init=lambda_init),
        out_shape=jax.ShapeDtypeStruct((B, S, DIFF_W), BF16),
        grid=(B, H, S // DIFF_Q_TILE),
        in_specs=[
            pl.BlockSpec((1, 1, HEAD_PAD, DIFF_Q_TILE), lambda b, h, i: (b, h, 0, i)),
            pl.BlockSpec((1, 1, S, HEAD_PAD), lambda b, h, i: (b, h, 0, 0)),
            pl.BlockSpec((1, 1, DIFF_VT_ROWS, S), lambda b, h, i: (b, h, 0, 0)),
            pl.BlockSpec((1, DIFF_Q_TILE, HEAD_PAD), lambda b, h, i: (b, i, h)),
            pl.BlockSpec((1, 1, DIFF_Q_TILE), lambda b, h, i: (b, 0, i)),
            pl.BlockSpec((1, S, 1), lambda b, h, i: (b, 0, 0)),
            pl.BlockSpec((1, 1, S), lambda b, h, i: (b, 0, 0)),
            pl.BlockSpec((1,) + k_sq_max.shape[1:], lambda b, h, i: (b, 0, 0, 0)),
            pl.BlockSpec((1, 1, LANES), lambda b, h, i: (h, 0, 0)),
            vec(DIFF_HD), vec(DIFF_HD), vec(DIFF_HD), vec(DIFF_HD),
            vec(DIFF_V),
        ],
        out_specs=pl.BlockSpec((1, DIFF_Q_TILE, HEAD_PAD), lambda b, h, i: (b, i, h)),
        scratch_shapes=[
            pltpu.VMEM((2, 2, DIFF_K_TILE, DIFF_Q_TILE), F32),
            pltpu.VMEM((2, 1, DIFF_Q_TILE), F32),
            pltpu.VMEM((2, DIFF_VT_ROWS, DIFF_Q_TILE), F32),
        ],
        compiler_params=_cparams(("parallel", "parallel", "parallel")),
        name="diff_attn",
    )(qt, k, vt, sg, pos_row, pos_col, pos_row, k_sq_max, slopes, lq1, lk1, lq2, lk2, hg)


def _post_kernel(o_ref, x_ref, mod_ref, wo_ref, fg_ref, out_ref, *, final):
    y = _dot(o_ref[0], wo_ref[...])
    gate = mod_ref[0, :, 2 * D_MODEL:]
    x_new = x_ref[0] + gate * y
    if final:
        x_new = _rms(x_new, fg_ref[...])
    out_ref[0] = x_new


def _post(o, x, mod_i, wo, fg, final):
    B, S, _ = x.shape
    T = POST_ROW_TILE
    const = lambda shape: pl.BlockSpec(shape, lambda b, t: (0,) * len(shape))
    return pl.pallas_call(
        functools.partial(_post_kernel, final=final),
        out_shape=jax.ShapeDtypeStruct((B, S, D_MODEL), F32),
        grid=(B, S // T),
        in_specs=[
            pl.BlockSpec((1, T, D_MODEL), lambda b, t: (b, t, 0)),
            pl.BlockSpec((1, T, D_MODEL), lambda b, t: (b, t, 0)),
            pl.BlockSpec((1, 1, 3 * D_MODEL), lambda b, t: (b, 0, 0)),
            const(wo.shape),
            const((1, D_MODEL)),
        ],
        out_specs=pl.BlockSpec((1, T, D_MODEL), lambda b, t: (b, t, 0)),
        compiler_params=_cparams(("parallel", "parallel")),
        name="post_final" if final else "post",
    )(o, x, mod_i, wo, fg)


def _rot_cols(w):
    half = MLA_ROPE // 2
    return jnp.concatenate([-w[..., half:], w[..., :half]], axis=-1)


def _pad_slab(w, lo):
    pad = [(0, 0)] * (w.ndim - 1) + [(lo, HEAD_PAD - lo - w.shape[-1])]
    return jnp.pad(w, pad)


def _mla_weights(w_in, w_q_up, w_kv_up):
    o1 = MLA_Q_RANK
    o2 = o1 + MLA_KV_RANK
    o3 = o2 + MLA_ROPE
    w_kr = w_in[:, o2:o3]
    win = jnp.concatenate([
        w_in[:, :o2],
        _pad_slab(w_kr, MLA_NOPE),
        _pad_slab(_rot_cols(w_kr), MLA_NOPE),
        w_in[:, o3:],
    ], axis=1).astype(BF16)
    wq3 = w_q_up.reshape(MLA_Q_RANK, MLA_HEADS, MLA_QK)
    wq_plain = _pad_slab(wq3, 0).reshape(MLA_Q_RANK, MLA_HEADS * HEAD_PAD)
    wq_rot = _pad_slab(_rot_cols(wq3[..., MLA_NOPE:]), MLA_NOPE).reshape(MLA_Q_RANK, MLA_HEADS * HEAD_PAD)
    wq = jnp.concatenate([wq_plain, wq_rot], axis=1).astype(BF16)
    wkv3 = w_kv_up.reshape(MLA_KV_RANK, MLA_HEADS, MLA_NOPE + MLA_V)
    wk = _pad_slab(wkv3[..., :MLA_NOPE], 0).reshape(MLA_KV_RANK, MLA_HEADS * HEAD_PAD)
    wv = _pad_slab(wkv3[..., MLA_NOPE:], 0).reshape(MLA_KV_RANK, MLA_HEADS * HEAD_PAD)
    wkv = jnp.concatenate([wk, wv], axis=1).astype(BF16)
    return win, wq, wkv


def kernel(x, c, positions, ada_w, ada_b, norm_g, mla_w_in, mla_q_norm_g, mla_w_q_up, mla_kv_norm_g,
           mla_w_kv_up, mla_w_o, diff_w_in, diff_lq1, diff_lk1, diff_lq2, diff_lk2, diff_head_g,
           diff_w_o, final_g):
    B, S, D = x.shape
    assert D == D_MODEL and all(S % t == 0 for t in (ROW_TILE, DIFF_PRE_ROW_TILE, POST_ROW_TILE, ROPE_ROW_TILE))
    assert S % MLA_Q_TILE == 0 and S % MLA_K_TILE == 0 and S % DIFF_Q_TILE == 0 and S % DIFF_K_TILE == 0
    c_pad = jnp.pad(c, ((0, 8 - B), (0, 0)))
    mod = _ada_mod(c_pad, ada_w, ada_b)[:, :B].reshape(DEPTH, B, 1, 3 * D)

    pos_col = positions.reshape(B, S, 1)
    pos_row = positions.reshape(B, 1, S)
    inv = ROPE_BASE ** (-jnp.arange(0, MLA_ROPE, 2, dtype=F32) / MLA_ROPE)
    cos, sin = _rope_tables(pos_row, jnp.concatenate([inv, inv]).reshape(MLA_ROPE, 1))
    slopes = jnp.exp2(-8.0 * jnp.arange(1, DIFF_HEADS + 1, dtype=F32) / DIFF_HEADS) * LOG2E
    slopes = jnp.broadcast_to(slopes[:, None, None], (DIFF_HEADS, 1, LANES))
    final_row = final_g.reshape(1, D)

    for i in range(DEPTH):
        j = i // 2
        g = norm_g[i].reshape(1, D)
        last = i == DEPTH - 1
        if i % 2 == 0:
            win, wq, wkv = _mla_weights(mla_w_in[j], mla_w_q_up[j], mla_w_kv_up[j])
            qt, k, vt, sg = _mla_pre(x, mod[i], g, cos, sin, win,
                                     mla_q_norm_g[j].reshape(1, -1), wq,
                                     mla_kv_norm_g[j].reshape(1, -1), wkv)
            o = _mla_attn(qt, k, vt, sg)
            wo = mla_w_o[j].astype(BF16)
        else:
            lambda_init = 0.8 - 0.6 * math.exp(-0.3 * i)
            qt, k, vt, sg, k_sq_max = _diff_pre(x, mod[i], g, diff_w_in[j].astype(BF16))
            o = _diff_attn(qt, k, vt, sg, pos_col, pos_row, k_sq_max, slopes,
                           diff_lq1[j].reshape(1, -1), diff_lk1[j].reshape(1, -1),
                           diff_lq2[j].reshape(1, -1), diff_lk2[j].reshape(1, -1),
                           diff_head_g[j].reshape(1, -1), lambda_init)
            wo = diff_w_o[j].astype(BF16)
        x = _post(o, x, mod[i], wo, final_row, last)
    return x
```

```python
import functools
import math

import jax
import jax.numpy as jnp
from jax import lax
from jax.experimental import pallas as pl
from jax.experimental.pallas import tpu as pltpu

D_MODEL = 1024
DEPTH = 4
EPS = 1e-6

MLA_HEADS = 16
MLA_NOPE = 64
MLA_ROPE = 32
MLA_V = 64
MLA_Q_RANK = 384
MLA_KV_RANK = 256
ROPE_BASE = 10000.0
MLA_QK = MLA_NOPE + MLA_ROPE

DIFF_HD = 64
DIFF_HEADS = D_MODEL // (2 * DIFF_HD)
DIFF_W = DIFF_HEADS * 2 * DIFF_HD
DIFF_V = 2 * DIFF_HD

LANES = 128
BF16_SUBLANES = 16
HEAD_PAD = LANES
LOG2E = math.log2(math.e)

MLA_VT_ROWS = MLA_V + BF16_SUBLANES
DIFF_VT_ROWS = DIFF_V + BF16_SUBLANES

MLA_PRE_ROW_TILE = 256
DIFF_PRE_ROW_TILE = 512
POST_ROW_TILE = 1024
ROPE_ROW_TILE = 512
MLA_Q_TILE = 512
MLA_K_TILE = 512
DIFF_Q_TILE = 512
DIFF_K_TILE = 256
MLA_STEPS_PER_TRIP = 4
DIFF_STEPS_PER_TRIP = 2
PRUNE_LOG2 = 150.0
NORM_MARGIN = 1.02
VMEM_LIMIT = 56 * 1024 * 1024

BF16 = jnp.bfloat16
F32 = jnp.float32


def _cparams(sem):
    return pltpu.CompilerParams(dimension_semantics=sem, vmem_limit_bytes=VMEM_LIMIT)


def _dot(a, b):
    return jnp.dot(a, b, preferred_element_type=F32)


def _rms(x, g):
    return x * lax.rsqrt(jnp.mean(x * x, axis=-1, keepdims=True) + EPS) * g


def _silu(x):
    return x * (1.0 / (1.0 + jnp.exp(-x)))


def _mod_kernel(c_ref, w_ref, b_ref, o_ref):
    c = c_ref[...]
    o_ref[0] = _dot(_silu(c).astype(BF16), w_ref[0].astype(BF16)) + b_ref[0]


def _ada_mod(c_pad, ada_w, ada_b):
    rows = c_pad.shape[0]
    nblk = 3
    return pl.pallas_call(
        _mod_kernel,
        out_shape=jax.ShapeDtypeStruct((DEPTH, rows, 3 * D_MODEL), F32),
        grid=(DEPTH, nblk),
        in_specs=[
            pl.BlockSpec((rows, D_MODEL), lambda i, n: (0, 0)),
            pl.BlockSpec((1, D_MODEL, D_MODEL), lambda i, n: (i, 0, n)),
            pl.BlockSpec((1, 1, D_MODEL), lambda i, n: (i, 0, n)),
        ],
        out_specs=pl.BlockSpec((1, rows, D_MODEL), lambda i, n: (i, 0, n)),
        compiler_params=_cparams(("arbitrary", "arbitrary")),
        name="ada_mod",
    )(c_pad, ada_w, ada_b.reshape(DEPTH, 1, 3 * D_MODEL))


def _modulated_norm(x_ref, mod_ref, g_ref):
    x = x_ref[0]
    shift = mod_ref[0, :, 0:D_MODEL]
    scale = mod_ref[0, :, D_MODEL:2 * D_MODEL]
    return _rms(x, g_ref[...]) * (1.0 + scale) + shift


def _pipelined_key_loop(scores_fn, values_fn, s_ref, m_ref, acc_ref, *, steps_per_trip,
                        first_trip, last_trip):
    def step(t_cons, slot, col_max, t_prod):
        thunks = scores_fn(t_prod) if t_prod is not None else None
        next_max = []
        for c, vt in enumerate(values_fn(t_cons)):
            if thunks is not None:
                s = thunks[c]()
                s_ref[1 - slot, c] = s
                next_max.append(jnp.max(s, axis=0, keepdims=True))
            m_prev = m_ref[c]
            m_new = jnp.maximum(m_prev, col_max[c])
            p = jnp.exp2(s_ref[slot, c] - m_new).astype(BF16)
            alpha = jnp.exp2(m_prev - m_new)
            m_ref[c] = m_new
            acc_ref[c] = alpha * acc_ref[c] + _dot(vt, p)
        return tuple(next_max)

    def trip(i, col_max, last):
        t0 = steps_per_trip * i
        for u in range(steps_per_trip):
            final_step = last and u == steps_per_trip - 1
            col_max = step(t0 + u, u % 2, col_max, None if final_step else t0 + u + 1)
        return col_max

    assert steps_per_trip % 2 == 0
    cm = []
    for c, thunk in enumerate(scores_fn(steps_per_trip * first_trip)):
        s = thunk()
        s_ref[0, c] = s
        cm.append(jnp.max(s, axis=0, keepdims=True))
    cm = lax.fori_loop(first_trip, last_trip, lambda i, cm: trip(i, cm, False), tuple(cm))
    trip(last_trip, cm, True)


def _rope_kernel(pos_ref, inv_ref, cos_out, sin_out):
    ang = inv_ref[...] * pos_ref[0].astype(F32)
    T = ang.shape[1]
    ones = jnp.ones((MLA_NOPE, T), F32)
    tail = HEAD_PAD - MLA_QK
    cos_out[0] = jnp.concatenate([ones, jnp.cos(ang), ones[:tail]], axis=0).T
    sin_out[0] = jnp.concatenate([0.0 * ones, jnp.sin(ang), 0.0 * ones[:tail]], axis=0).T


def _rope_tables(pos_row, inv_col):
    B, _, S = pos_row.shape
    T = ROPE_ROW_TILE
    table = jax.ShapeDtypeStruct((B, S, HEAD_PAD), F32)
    return pl.pallas_call(
        _rope_kernel,
        out_shape=(table, table),
        grid=(B, S // T),
        in_specs=[
            pl.BlockSpec((1, 1, T), lambda b, t: (b, 0, t)),
            pl.BlockSpec((MLA_ROPE, 1), lambda b, t: (0, 0)),
        ],
        out_specs=(
            pl.BlockSpec((1, T, HEAD_PAD), lambda b, t: (b, t, 0)),
            pl.BlockSpec((1, T, HEAD_PAD), lambda b, t: (b, t, 0)),
        ),
        compiler_params=_cparams(("parallel", "parallel")),
        name="rope_tables",
    )(pos_row, inv_col)


def _mla_pre_kernel(x_ref, mod_ref, g_ref, cos_ref, sin_ref, win_ref, qg_ref, wq_ref,
                    kvg_ref, wkv_ref, qt_out, k_out, vt_out, sg_out):
    h = _modulated_norm(x_ref, mod_ref, g_ref).astype(BF16)
    proj = _dot(h, win_ref[...])
    o1 = MLA_Q_RANK
    o2 = o1 + MLA_KV_RANK
    o3 = o2 + HEAD_PAD
    o4 = o3 + HEAD_PAD
    qn = _rms(proj[:, :o1], qg_ref[...]).astype(BF16)
    kvn = _rms(proj[:, o1:o2], kvg_ref[...]).astype(BF16)
    cos = cos_ref[0]
    sin = sin_ref[0]
    k_rope = proj[:, o2:o3] * cos + proj[:, o3:o4] * sin
    sg_out[0] = _silu(proj[:, o4:])

    qq = _dot(qn, wq_ref[...])
    kv = _dot(kvn, wkv_ref[...])
    half = MLA_HEADS * HEAD_PAD
    qscale = (MLA_QK ** -0.5) * LOG2E
    lane = lax.broadcasted_iota(jnp.int32, (1, HEAD_PAD), 1)
    ones_col = (lane == MLA_V).astype(F32)
    for hd in range(MLA_HEADS):
        a = hd * HEAD_PAD
        b = a + HEAD_PAD
        q_h = (qq[:, a:b] * cos + qq[:, half + a:half + b] * sin) * qscale
        qt_out[0, hd] = q_h.T.astype(BF16)
        k_out[0, hd] = (kv[:, a:b] + k_rope).astype(BF16)
        v_h = kv[:, half + a:half + b] + ones_col
        vt_out[0, hd] = v_h.T[:MLA_VT_ROWS].astype(BF16)


def _mla_pre(x, mod_i, g, cos, sin, win, qg, wq, kvg, wkv):
    B, S, _ = x.shape
    T = MLA_PRE_ROW_TILE
    const = lambda shape: pl.BlockSpec(shape, lambda b, t: (0,) * len(shape))
    return pl.pallas_call(
        _mla_pre_kernel,
        out_shape=(
            jax.ShapeDtypeStruct((B, MLA_HEADS, HEAD_PAD, S), BF16),
            jax.ShapeDtypeStruct((B, MLA_HEADS, S, HEAD_PAD), BF16),
            jax.ShapeDtypeStruct((B, MLA_HEADS, MLA_VT_ROWS, S), BF16),
            jax.ShapeDtypeStruct((B, S, D_MODEL), F32),
        ),
        grid=(B, S // T),
        in_specs=[
            pl.BlockSpec((1, T, D_MODEL), lambda b, t: (b, t, 0)),
            pl.BlockSpec((1, 1, 3 * D_MODEL), lambda b, t: (b, 0, 0)),
            const((1, D_MODEL)),
            pl.BlockSpec((1, T, HEAD_PAD), lambda b, t: (b, t, 0)),
            pl.BlockSpec((1, T, HEAD_PAD), lambda b, t: (b, t, 0)),
            const(win.shape),
            const((1, MLA_Q_RANK)),
            const(wq.shape),
            const((1, MLA_KV_RANK)),
            const(wkv.shape),
        ],
        out_specs=(
            pl.BlockSpec((1, MLA_HEADS, HEAD_PAD, T), lambda b, t: (b, 0, 0, t)),
            pl.BlockSpec((1, MLA_HEADS, T, HEAD_PAD), lambda b, t: (b, 0, t, 0)),
            pl.BlockSpec((1, MLA_HEADS, MLA_VT_ROWS, T), lambda b, t: (b, 0, 0, t)),
            pl.BlockSpec((1, T, D_MODEL), lambda b, t: (b, t, 0)),
        ),
        compiler_params=_cparams(("parallel", "parallel")),
        name="mla_pre",
    )(x, mod_i, g, cos, sin, win, qg, wq, kvg, wkv)


MLA_HEADS_PER_STEP = 4


def _key_slice(t, tile):
    return pl.ds(pl.multiple_of(t * tile, tile), tile)


def _key_scores(k_ref, qt_ref, hh, t):
    return _dot(k_ref[0, hh, _key_slice(t, MLA_K_TILE), :], qt_ref[0, hh])


def _mla_attn_kernel(qt_ref, k_ref, vt_ref, sg_ref, o_ref, s_ref, m_ref, acc_ref):
    n_kv = k_ref.shape[2] // MLA_K_TILE
    m_ref[...] = jnp.full(m_ref.shape, -jnp.inf, F32)
    acc_ref[...] = jnp.zeros(acc_ref.shape, F32)

    heads = range(MLA_HEADS_PER_STEP)

    def scores_fn(t):
        return [functools.partial(_key_scores, k_ref, qt_ref, hh, t) for hh in heads]

    def values_fn(t):
        return [vt_ref[0, hh, :, _key_slice(t, MLA_K_TILE)] for hh in heads]

    assert n_kv % MLA_STEPS_PER_TRIP == 0
    _pipelined_key_loop(scores_fn, values_fn, s_ref, m_ref, acc_ref,
                        steps_per_trip=MLA_STEPS_PER_TRIP, first_trip=0,
                        last_trip=n_kv // MLA_STEPS_PER_TRIP - 1)

    outs = []
    for hh in range(MLA_HEADS_PER_STEP):
        acc = acc_ref[hh]
        outs.append(acc[:MLA_V] / acc[MLA_V:MLA_V + 1])
    o = jnp.concatenate(outs, axis=0).T
    o_ref[0] = (o * sg_ref[0]).astype(BF16)


def _mla_attn(qt, k, vt, sg):
    B, H, S, _ = k.shape
    G = MLA_HEADS_PER_STEP
    return pl.pallas_call(
        _mla_attn_kernel,
        out_shape=jax.ShapeDtypeStruct((B, S, H * MLA_V), BF16),
        grid=(B, H // G, S // MLA_Q_TILE),
        in_specs=[
            pl.BlockSpec((1, G, HEAD_PAD, MLA_Q_TILE), lambda b, h, i: (b, h, 0, i)),
            pl.BlockSpec((1, G, S, HEAD_PAD), lambda b, h, i: (b, h, 0, 0)),
            pl.BlockSpec((1, G, MLA_VT_ROWS, S), lambda b, h, i: (b, h, 0, 0)),
            pl.BlockSpec((1, MLA_Q_TILE, G * MLA_V), lambda b, h, i: (b, i, h)),
        ],
        out_specs=pl.BlockSpec((1, MLA_Q_TILE, G * MLA_V), lambda b, h, i: (b, i, h)),
        scratch_shapes=[
            pltpu.VMEM((2, G, MLA_K_TILE, MLA_Q_TILE), F32),
            pltpu.VMEM((G, 1, MLA_Q_TILE), F32),
            pltpu.VMEM((G, MLA_VT_ROWS, MLA_Q_TILE), F32),
        ],
        compiler_params=_cparams(("parallel", "parallel", "parallel")),
        name="mla_attn",
    )(qt, k, vt, sg)


def _diff_pre_kernel(x_ref, mod_ref, g_ref, win_ref, qt_out, k_out, vt_out, sg_out, kn_out):
    h = _modulated_norm(x_ref, mod_ref, g_ref).astype(BF16)
    proj = _dot(h, win_ref[...])
    qscale = (DIFF_HD ** -0.5) * LOG2E
    sg_out[0] = _silu(proj[:, 3 * DIFF_W:])
    T = proj.shape[0]
    sub = lax.broadcasted_iota(jnp.int32, (BF16_SUBLANES, T), 0)
    ones_rows = (sub == 0).astype(F32)
    k_sq_max = []
    for hd in range(DIFF_HEADS):
        a = hd * HEAD_PAD
        b = a + HEAD_PAD
        qt_out[0, hd] = (proj[:, a:b] * qscale).T.astype(BF16)
        k_h = proj[:, DIFF_W + a:DIFF_W + b]
        k_out[0, hd] = k_h.astype(BF16)
        k_sq = jnp.sum(k_h * k_h, axis=-1, keepdims=True)
        k_sq_max.append(jnp.broadcast_to(jnp.max(k_sq, axis=0, keepdims=True), (1, LANES)))
        v_t = proj[:, 2 * DIFF_W + a:2 * DIFF_W + b].T
        vt_out[0, hd] = jnp.concatenate([v_t, ones_rows], axis=0).astype(BF16)
    kn_out[0, 0] = jnp.concatenate(k_sq_max, axis=0)


def _diff_pre(x, mod_i, g, win):
    B, S, _ = x.shape
    T = DIFF_PRE_ROW_TILE
    const = lambda shape: pl.BlockSpec(shape, lambda b, t: (0,) * len(shape))
    return pl.pallas_call(
        _diff_pre_kernel,
        out_shape=(
            jax.ShapeDtypeStruct((B, DIFF_HEADS, HEAD_PAD, S), BF16),
            jax.ShapeDtypeStruct((B, DIFF_HEADS, S, HEAD_PAD), BF16),
            jax.ShapeDtypeStruct((B, DIFF_HEADS, DIFF_VT_ROWS, S), BF16),
            jax.ShapeDtypeStruct((B, S, D_MODEL), F32),
            jax.ShapeDtypeStruct((B, S // T, DIFF_HEADS, LANES), F32),
        ),
        grid=(B, S // T),
        in_specs=[
            pl.BlockSpec((1, T, D_MODEL), lambda b, t: (b, t, 0)),
            pl.BlockSpec((1, 1, 3 * D_MODEL), lambda b, t: (b, 0, 0)),
            const((1, D_MODEL)),
            const(win.shape),
        ],
        out_specs=(
            pl.BlockSpec((1, DIFF_HEADS, HEAD_PAD, T), lambda b, t: (b, 0, 0, t)),
            pl.BlockSpec((1, DIFF_HEADS, T, HEAD_PAD), lambda b, t: (b, 0, t, 0)),
            pl.BlockSpec((1, DIFF_HEADS, DIFF_VT_ROWS, T), lambda b, t: (b, 0, 0, t)),
            pl.BlockSpec((1, T, D_MODEL), lambda b, t: (b, t, 0)),
            pl.BlockSpec((1, 1, DIFF_HEADS, LANES), lambda b, t: (b, t, 0, 0)),
        ),
        compiler_params=_cparams(("parallel", "parallel")),
        name="diff_pre",
    )(x, mod_i, g, win)


def _needed_trips(qt, posq, posrow_ref, kn_ref, slope, n_trips, keys_per_trip):
    qf = qt.astype(F32)
    q_sq = jnp.max(jnp.sum(qf * qf, axis=0, keepdims=True))
    score_bound = jnp.sqrt(q_sq * jnp.max(kn_ref[0])) * NORM_MARGIN
    limit = PRUNE_LOG2 + 2.0 * score_bound
    q_lo = jnp.min(posq)
    q_hi = jnp.max(posq)
    first = jnp.int32(n_trips - 1)
    last = jnp.int32(0)
    for p in range(n_trips):
        pk = posrow_ref[0, :, p * keys_per_trip:(p + 1) * keys_per_trip]
        gap = jnp.maximum(jnp.maximum(jnp.min(pk) - q_hi, q_lo - jnp.max(pk)), 0)
        needed = gap.astype(F32) * slope <= limit
        first = jnp.where(needed, jnp.minimum(first, p), first)
        last = jnp.where(needed, jnp.maximum(last, p), last)
    return first, jnp.maximum(last, first)


def _diff_attn_kernel(qt_ref, k_ref, vt_ref, sg_ref, posq_ref, posk_ref, posrow_ref, kn_ref,
                      slope_ref, lq1_ref, lk1_ref, lq2_ref, lk2_ref, hg_ref, o_ref,
                      s_ref, m_ref, acc_ref, *, lambda_init):
    n_kv = k_ref.shape[2] // DIFF_K_TILE
    m_ref[...] = jnp.full(m_ref.shape, -jnp.inf, F32)
    acc_ref[...] = jnp.zeros(acc_ref.shape, F32)

    qt = qt_ref[0, 0]
    row = lax.broadcasted_iota(jnp.int32, qt.shape, 0)
    zero = jnp.zeros_like(qt)
    qt_maps = (jnp.where(row < DIFF_HD, qt, zero), jnp.where(row >= DIFF_HD, qt, zero))
    slope = slope_ref[0][:, 0:1]
    pos_q = posq_ref[0].astype(F32) * slope

    def scores_fn(t):
        keys = _key_slice(t, DIFF_K_TILE)
        pos_k = posk_ref[0, keys, :].astype(F32) * slope
        bias = jnp.abs(pos_k - pos_q)
        kk = k_ref[0, 0, keys, :]
        return [lambda qt_c=qt_c: _dot(kk, qt_c) - bias for qt_c in qt_maps]

    def values_fn(t):
        vt = vt_ref[0, 0, :, _key_slice(t, DIFF_K_TILE)]
        return [vt, vt]

    assert n_kv % DIFF_STEPS_PER_TRIP == 0
    first_trip, last_trip = _needed_trips(
        qt, posq_ref[0], posrow_ref, kn_ref, jnp.max(slope_ref[0]),
        n_kv // DIFF_STEPS_PER_TRIP, DIFF_STEPS_PER_TRIP * DIFF_K_TILE)
    _pipelined_key_loop(scores_fn, values_fn, s_ref, m_ref, acc_ref,
                        steps_per_trip=DIFF_STEPS_PER_TRIP, first_trip=first_trip,
                        last_trip=last_trip)

    lam =(jnp.exp(jnp.sum(lq1_ref[...] * lk1_ref[...], axis=-1, keepdims=True))
           - jnp.exp(jnp.sum(lq2_ref[...] * lk2_ref[...], axis=-1, keepdims=True))
           + lambda_init)
    a1 = acc_ref[0]
    a2 = acc_ref[1]
    o = a1[:DIFF_V] / a1[DIFF_V:DIFF_V + 1] - lam * (a2[:DIFF_V] / a2[DIFF_V:DIFF_V + 1])
    o = o * lax.rsqrt(jnp.mean(o * o, axis=0, keepdims=True) + EPS)
    o = o.T * (hg_ref[...] * (1.0 - lambda_init))
    o_ref[0] = (o * sg_ref[0]).astype(BF16)


def _diff_attn(qt, k, vt, sg, pos_col, pos_row, k_sq_max, slopes, lq1, lk1, lq2, lk2, hg, lambda_init):
    B, H, S, _ = k.shape
    vec = lambda n: pl.BlockSpec((1, n), lambda b, h, i: (0, 0))
    return pl.pallas_call(
        functools.partial(_diff_attn_kernel, lambda_init=lambda_init),
        out_shape=jax.ShapeDtypeStruct((B, S, DIFF_W), BF16),
        grid=(B, H, S // DIFF_Q_TILE),
        in_specs=[
            pl.BlockSpec((1, 1, HEAD_PAD, DIFF_Q_TILE), lambda b, h, i: (b, h, 0, i)),
            pl.BlockSpec((1, 1, S, HEAD_PAD), lambda b, h, i: (b, h, 0, 0)),
            pl.BlockSpec((1, 1, DIFF_VT_ROWS, S), lambda b, h, i: (b, h, 0, 0)),
            pl.BlockSpec((1, DIFF_Q_TILE, HEAD_PAD), lambda b, h, i: (b, i, h)),
            pl.BlockSpec((1, 1, DIFF_Q_TILE), lambda b, h, i: (b, 0, i)),
            pl.BlockSpec((1, S, 1), lambda b, h, i: (b, 0, 0)),
            pl.BlockSpec((1, 1, S), lambda b, h, i: (b, 0, 0)),
            pl.BlockSpec((1,) + k_sq_max.shape[1:], lambda b, h, i: (b, 0, 0, 0)),
            pl.BlockSpec((1, 1, LANES), lambda b, h, i: (h, 0, 0)),
            vec(DIFF_HD), vec(DIFF_HD), vec(DIFF_HD), vec(DIFF_HD),
            vec(DIFF_V),
        ],
        out_specs=pl.BlockSpec((1, DIFF_Q_TILE, HEAD_PAD), lambda b, h, i: (b, i, h)),
        scratch_shapes=[
            pltpu.VMEM((2, 2, DIFF_K_TILE, DIFF_Q_TILE), F32),
            pltpu.VMEM((2, 1, DIFF_Q_TILE), F32),
            pltpu.VMEM((2, DIFF_VT_ROWS, DIFF_Q_TILE), F32),
        ],
        compiler_params=_cparams(("parallel", "parallel", "parallel")),
        name="diff_attn",
    )(qt, k, vt, sg, pos_row, pos_col, pos_row, k_sq_max, slopes, lq1, lk1, lq2, lk2, hg)


def _post_kernel(o_ref, x_ref, mod_ref, wo_ref, fg_ref, out_ref, *, final):
    y = _dot(o_ref[0], wo_ref[...])
    gate = mod_ref[0, :, 2 * D_MODEL:]
    x_new = x_ref[0] + gate * y
    if final:
        x_new = _rms(x_new, fg_ref[...])
    out_ref[0] = x_new


def _post(o, x, mod_i, wo, fg, final):
    B, S, _ = x.shape
    T = POST_ROW_TILE
    const = lambda shape: pl.BlockSpec(shape, lambda b, t: (0,) * len(shape))
    return pl.pallas_call(
        functools.partial(_post_kernel, final=final),
        out_shape=jax.ShapeDtypeStruct((B, S, D_MODEL), F32),
        grid=(B, S // T),
        in_specs=[
            pl.BlockSpec((1, T, D_MODEL), lambda b, t: (b, t, 0)),
            pl.BlockSpec((1, T, D_MODEL), lambda b, t: (b, t, 0)),
            pl.BlockSpec((1, 1, 3 * D_MODEL), lambda b, t: (b, 0, 0)),
            const(wo.shape),
            const((1, D_MODEL)),
        ],
        out_specs=pl.BlockSpec((1, T, D_MODEL), lambda b, t: (b, t, 0)),
        compiler_params=_cparams(("parallel", "parallel")),
        name="post_final" if final else "post",
    )(o, x, mod_i, wo, fg)


def _rot_cols(w):
    half = MLA_ROPE // 2
    return jnp.concatenate([-w[..., half:], w[..., :half]], axis=-1)


def _pad_slab(w, lo):
    pad = [(0, 0)] * (w.ndim - 1) + [(lo, HEAD_PAD - lo - w.shape[-1])]
    return jnp.pad(w, pad)


def _mla_weights(w_in, w_q_up, w_kv_up):
    o1 = MLA_Q_RANK
    o2 = o1 + MLA_KV_RANK
    o3 = o2 + MLA_ROPE
    w_kr = w_in[:, o2:o3]
    win = jnp.concatenate([
        w_in[:, :o2],
        _pad_slab(w_kr, MLA_NOPE),
        _pad_slab(_rot_cols(w_kr), MLA_NOPE),
        w_in[:, o3:],
    ], axis=1).astype(BF16)
    wq3 = w_q_up.reshape(MLA_Q_RANK, MLA_HEADS, MLA_QK)
    wq_plain = _pad_slab(wq3, 0).reshape(MLA_Q_RANK, MLA_HEADS * HEAD_PAD)
    wq_rot = _pad_slab(_rot_cols(wq3[..., MLA_NOPE:]), MLA_NOPE).reshape(MLA_Q_RANK, MLA_HEADS * HEAD_PAD)
    wq = jnp.concatenate([wq_plain, wq_rot], axis=1).astype(BF16)
    wkv3 = w_kv_up.reshape(MLA_KV_RANK, MLA_HEADS, MLA_NOPE + MLA_V)
    wk = _pad_slab(wkv3[..., :MLA_NOPE], 0).reshape(MLA_KV_RANK, MLA_HEADS * HEAD_PAD)
    wv = _pad_slab(wkv3[..., MLA_NOPE:], 0).reshape(MLA_KV_RANK, MLA_HEADS * HEAD_PAD)
    wkv = jnp.concatenate([wk, wv], axis=1).astype(BF16)
    return win, wq, wkv


def kernel(x, c, positions, ada_w, ada_b, norm_g, mla_w_in, mla_q_norm_g, mla_w_q_up, mla_kv_norm_g,
           mla_w_kv_up, mla_w_o, diff_w_in, diff_lq1, diff_lk1, diff_lq2, diff_lk2, diff_head_g,
           diff_w_o, final_g):
    B, S, D = x.shape
    tiles = (MLA_PRE_ROW_TILE, DIFF_PRE_ROW_TILE, POST_ROW_TILE, ROPE_ROW_TILE,
             MLA_Q_TILE, MLA_K_TILE, DIFF_Q_TILE, DIFF_K_TILE)
    assert D == D_MODEL and all(S % t == 0 for t in tiles)
    sublanes = 8
    assert B <= sublanes
    c_pad = jnp.pad(c, ((0, sublanes - B), (0, 0)))
    mod = _ada_mod(c_pad, ada_w, ada_b)[:, :B].reshape(DEPTH, B, 1, 3 * D)

    pos_col = positions.reshape(B, S, 1)
    pos_row = positions.reshape(B, 1, S)
    inv = ROPE_BASE ** (-jnp.arange(0, MLA_ROPE, 2, dtype=F32) / MLA_ROPE)
    cos, sin = _rope_tables(pos_row, jnp.concatenate([inv, inv]).reshape(MLA_ROPE, 1))
    slopes = jnp.exp2(-8.0 * jnp.arange(1, DIFF_HEADS + 1, dtype=F32) / DIFF_HEADS) * LOG2E
    slopes = jnp.broadcast_to(slopes[:, None, None], (DIFF_HEADS, 1, LANES))
    final_row = final_g.reshape(1, D)

    for i in range(DEPTH):
        j = i // 2
        g = norm_g[i].reshape(1, D)
        last = i == DEPTH - 1
        if i % 2 == 0:
            win, wq, wkv = _mla_weights(mla_w_in[j], mla_w_q_up[j], mla_w_kv_up[j])
            qt, k, vt, sg = _mla_pre(x, mod[i], g, cos, sin, win,
                                     mla_q_norm_g[j].reshape(1, -1), wq,
                                     mla_kv_norm_g[j].reshape(1, -1), wkv)
            o = _mla_attn(qt, k, vt, sg)
            wo = mla_w_o[j].astype(BF16)
        else:
            lambda_init = 0.8 - 0.6 * math.exp(-0.3 * i)
            qt, k, vt, sg, k_sq_max = _diff_pre(x, mod[i], g, diff_w_in[j].astype(BF16))
            o = _diff_attn(qt, k, vt, sg, pos_col, pos_row, k_sq_max, slopes,
                           diff_lq1[j].reshape(1, -1), diff_lk1[j].reshape(1, -1),
                           diff_lq2[j].reshape(1, -1), diff_lk2[j].reshape(1, -1),
                           diff_head_g[j].reshape(1, -1), lambda_init)
            wo = diff_w_o[j].astype(BF16)
        x = _post(o, x, mod[i], wo, final_row, last)
    return x
```

```python
import functools
import math

import jax
import jax.numpy as jnp
from jax import lax
from jax.experimental import pallas as pl
from jax.experimental.pallas import tpu as pltpu

D_MODEL = 1024
DEPTH = 4
EPS = 1e-6

MLA_HEADS = 16
MLA_NOPE = 64
MLA_ROPE = 32
MLA_V = 64
MLA_Q_RANK = 384
MLA_KV_RANK = 256
ROPE_BASE = 10000.0
MLA_QK = MLA_NOPE + MLA_ROPE

DIFF_HD = 64
DIFF_HEADS = D_MODEL // (2 * DIFF_HD)
DIFF_W = DIFF_HEADS * 2 * DIFF_HD
DIFF_V = 2 * DIFF_HD

LANES = 128
BF16_SUBLANES = 16
HEAD_PAD = LANES
LOG2E = math.log2(math.e)

MLA_VT_ROWS = MLA_V + BF16_SUBLANES
DIFF_VT_ROWS = DIFF_V + BF16_SUBLANES

MLA_PRE_ROW_TILE = 256
DIFF_PRE_ROW_TILE = 512
POST_ROW_TILE = 1024
ROPE_ROW_TILE = 512
MLA_Q_TILE = 512
MLA_K_TILE = 512
DIFF_Q_TILE = 512
DIFF_K_TILE = 256
MLA_STEPS_PER_TRIP = 4
DIFF_STEPS_PER_TRIP = 2
PRUNE_LOG2 = 150.0
NORM_MARGIN = 1.02
VMEM_LIMIT = 56 * 1024 * 1024

BF16 = jnp.bfloat16
F32 = jnp.float32


def _cparams(sem):
    return pltpu.CompilerParams(dimension_semantics=sem, vmem_limit_bytes=VMEM_LIMIT)


def _dot(a, b):
    return jnp.dot(a, b, preferred_element_type=F32)


def _rms(x, g):
    return x * lax.rsqrt(jnp.mean(x * x, axis=-1, keepdims=True) + EPS) * g


def _silu(x):
    return x * (1.0 / (1.0 + jnp.exp(-x)))


def _mod_kernel(c_ref, w_ref, b_ref, o_ref):
    c = c_ref[...]
    o_ref[0] = _dot(_silu(c).astype(BF16), w_ref[0].astype(BF16)) + b_ref[0]


def _ada_mod(c_pad, ada_w, ada_b):
    rows = c_pad.shape[0]
    nblk = 3
    return pl.pallas_call(
        _mod_kernel,
        out_shape=jax.ShapeDtypeStruct((DEPTH, rows, 3 * D_MODEL), F32),
        grid=(DEPTH, nblk),
        in_specs=[
            pl.BlockSpec((rows, D_MODEL), lambda i, n: (0, 0)),
            pl.BlockSpec((1, D_MODEL, D_MODEL), lambda i, n: (i, 0, n)),
            pl.BlockSpec((1, 1, D_MODEL), lambda i, n: (i, 0, n)),
        ],
        out_specs=pl.BlockSpec((1, rows, D_MODEL), lambda i, n: (i, 0, n)),
        compiler_params=_cparams(("arbitrary", "arbitrary")),
        name="ada_mod",
    )(c_pad, ada_w, ada_b.reshape(DEPTH, 1, 3 * D_MODEL))


def _modulated_norm(x_ref, mod_ref, g_ref):
    x = x_ref[0]
    shift = mod_ref[0, :, 0:D_MODEL]
    scale = mod_ref[0, :, D_MODEL:2 * D_MODEL]
    return _rms(x, g_ref[...]) * (1.0 + scale) + shift


def _pipelined_key_loop(scores_fn, values_fn, s_ref, m_ref, acc_ref, *, steps_per_trip,
                        first_trip, last_trip):
    def step(t_cons, slot, col_max, t_prod):
        thunks = scores_fn(t_prod) if t_prod is not None else None
        next_max = []
        for c, vt in enumerate(values_fn(t_cons)):
            if thunks is not None:
                s = thunks[c]()
                s_ref[1 - slot, c] = s
                next_max.append(jnp.max(s, axis=0, keepdims=True))
            m_prev = m_ref[c]
            m_new = jnp.maximum(m_prev, col_max[c])
            p = jnp.exp2(s_ref[slot, c] - m_new).astype(BF16)
            alpha = jnp.exp2(m_prev - m_new)
            m_ref[c] = m_new
            acc_ref[c] = alpha * acc_ref[c] + _dot(vt, p)
        return tuple(next_max)

    def trip(i, col_max, last):
        t0 = steps_per_trip * i
        for u in range(steps_per_trip):
            final_step = last and u == steps_per_trip - 1
            col_max = step(t0 + u, u % 2, col_max, None if final_step else t0 + u + 1)
        return col_max

    assert steps_per_trip % 2 == 0
    cm = []
    for c, thunk in enumerate(scores_fn(steps_per_trip * first_trip)):
        s = thunk()
        s_ref[0, c] = s
        cm.append(jnp.max(s, axis=0, keepdims=True))
    cm = lax.fori_loop(first_trip, last_trip, lambda i, cm: trip(i, cm, False), tuple(cm))
    trip(last_trip, cm, True)


def _rope_kernel(pos_ref, inv_ref, cos_out, sin_out):
    ang = inv_ref[...] * pos_ref[0].astype(F32)
    T = ang.shape[1]
    ones = jnp.ones((MLA_NOPE, T), F32)
    tail = HEAD_PAD - MLA_QK
    cos_out[0] = jnp.concatenate([ones, jnp.cos(ang), ones[:tail]], axis=0).T
    sin_out[0] = jnp.concatenate([0.0 * ones, jnp.sin(ang), 0.0 * ones[:tail]], axis=0).T


def _rope_tables(pos_row, inv_col):
    B, _, S = pos_row.shape
    T = ROPE_ROW_TILE
    table = jax.ShapeDtypeStruct((B, S, HEAD_PAD), F32)
    return pl.pallas_call(
        _rope_kernel,
        out_shape=(table, table),
        grid=(B, S // T),
        in_specs=[
            pl.BlockSpec((1, 1, T), lambda b, t: (b, 0, t)),
            pl.BlockSpec((MLA_ROPE, 1), lambda b, t: (0, 0)),
        ],
        out_specs=(
            pl.BlockSpec((1, T, HEAD_PAD), lambda b, t: (b, t, 0)),
            pl.BlockSpec((1, T, HEAD_PAD), lambda b, t: (b, t, 0)),
        ),
        compiler_params=_cparams(("parallel", "parallel")),
        name="rope_tables",
    )(pos_row, inv_col)


def _mla_pre_kernel(x_ref, mod_ref, g_ref, cos_ref, sin_ref, win_ref, qg_ref, wq_ref,
                    kvg_ref, wkv_ref, qt_out, k_out, vt_out, sg_out):
    h = _modulated_norm(x_ref, mod_ref, g_ref).astype(BF16)
    proj = _dot(h, win_ref[...])
    o1 = MLA_Q_RANK
    o2 = o1 + MLA_KV_RANK
    o3 = o2 + HEAD_PAD
    o4 = o3 + HEAD_PAD
    qn = _rms(proj[:, :o1], qg_ref[...]).astype(BF16)
    kvn = _rms(proj[:, o1:o2], kvg_ref[...]).astype(BF16)
    cos = cos_ref[0]
    sin = sin_ref[0]
    k_rope = proj[:, o2:o3] * cos + proj[:, o3:o4] * sin
    sg_out[0] = _silu(proj[:, o4:])

    qq = _dot(qn, wq_ref[...])
    kv = _dot(kvn, wkv_ref[...])
    half = MLA_HEADS * HEAD_PAD
    qscale = (MLA_QK ** -0.5) * LOG2E
    lane = lax.broadcasted_iota(jnp.int32, (1, HEAD_PAD), 1)
    ones_col = (lane == MLA_V).astype(F32)
    for hd in range(MLA_HEADS):
        a = hd * HEAD_PAD
        b = a + HEAD_PAD
        q_h = (qq[:, a:b] * cos + qq[:, half + a:half + b] * sin) * qscale
        qt_out[0, hd] = q_h.T.astype(BF16)
        k_out[0, hd] = (kv[:, a:b] + k_rope).astype(BF16)
        v_h = kv[:, half + a:half + b] + ones_col
        vt_out[0, hd] = v_h.T[:MLA_VT_ROWS].astype(BF16)


def _mla_pre(x, mod_i, g, cos, sin, win, qg, wq, kvg, wkv):
    B, S, _ = x.shape
    T = MLA_PRE_ROW_TILE
    const = lambda shape: pl.BlockSpec(shape, lambda b, t: (0,) * len(shape))
    return pl.pallas_call(
        _mla_pre_kernel,
        out_shape=(
            jax.ShapeDtypeStruct((B, MLA_HEADS, HEAD_PAD, S), BF16),
            jax.ShapeDtypeStruct((B, MLA_HEADS, S, HEAD_PAD), BF16),
            jax.ShapeDtypeStruct((B, MLA_HEADS, MLA_VT_ROWS, S), BF16),
            jax.ShapeDtypeStruct((B, S, D_MODEL), F32),
        ),
        grid=(B, S // T),
        in_specs=[
            pl.BlockSpec((1, T, D_MODEL), lambda b, t: (b, t, 0)),
            pl.BlockSpec((1, 1, 3 * D_MODEL), lambda b, t: (b, 0, 0)),
            const((1, D_MODEL)),
            pl.BlockSpec((1, T, HEAD_PAD), lambda b, t: (b, t, 0)),
            pl.BlockSpec((1, T, HEAD_PAD), lambda b, t: (b, t, 0)),
            const(win.shape),
            const((1, MLA_Q_RANK)),
            const(wq.shape),
            const((1, MLA_KV_RANK)),
            const(wkv.shape),
        ],
        out_specs=(
            pl.BlockSpec((1, MLA_HEADS, HEAD_PAD, T), lambda b, t: (b, 0, 0, t)),
            pl.BlockSpec((1, MLA_HEADS, T, HEAD_PAD), lambda b, t: (b, 0, t, 0)),
            pl.BlockSpec((1, MLA_HEADS, MLA_VT_ROWS, T), lambda b, t: (b, 0, 0, t)),
            pl.BlockSpec((1, T, D_MODEL), lambda b, t: (b, t, 0)),
        ),
        compiler_params=_cparams(("parallel", "parallel")),
        name="mla_pre",
    )(x, mod_i, g, cos, sin, win, qg, wq, kvg, wkv)


MLA_HEADS_PER_STEP = 4


def _key_slice(t, tile):
    return pl.ds(pl.multiple_of(t * tile, tile), tile)


def _key_scores(k_ref, qt_ref, hh, t):
    return _dot(k_ref[0, hh, _key_slice(t, MLA_K_TILE), :], qt_ref[0, hh])


def _mla_attn_kernel(qt_ref, k_ref, vt_ref, sg_ref, o_ref, s_ref, m_ref, acc_ref):
    n_kv = k_ref.shape[2] // MLA_K_TILE
    m_ref[...] = jnp.full(m_ref.shape, -jnp.inf, F32)
    acc_ref[...] = jnp.zeros(acc_ref.shape, F32)

    heads = range(MLA_HEADS_PER_STEP)

    def scores_fn(t):
        return [functools.partial(_key_scores, k_ref, qt_ref, hh, t) for hh in heads]

    def values_fn(t):
        return [vt_ref[0, hh, :, _key_slice(t, MLA_K_TILE)] for hh in heads]

    assert n_kv % MLA_STEPS_PER_TRIP == 0
    _pipelined_key_loop(scores_fn, values_fn, s_ref, m_ref, acc_ref,
                        steps_per_trip=MLA_STEPS_PER_TRIP, first_trip=0,
                        last_trip=n_kv // MLA_STEPS_PER_TRIP - 1)

    outs = []
    for hh in range(MLA_HEADS_PER_STEP):
        acc = acc_ref[hh]
        outs.append(acc[:MLA_V] / acc[MLA_V:MLA_V + 1])
    o = jnp.concatenate(outs, axis=0).T
    o_ref[0] = (o * sg_ref[0]).astype(BF16)


def _mla_attn(qt, k, vt, sg):
    B, H, S, _ = k.shape
    G = MLA_HEADS_PER_STEP
    return pl.pallas_call(
        _mla_attn_kernel,
        out_shape=jax.ShapeDtypeStruct((B, S, H * MLA_V), BF16),
        grid=(B, H // G, S // MLA_Q_TILE),
        in_specs=[
            pl.BlockSpec((1, G, HEAD_PAD, MLA_Q_TILE), lambda b, h, i: (b, h, 0, i)),
            pl.BlockSpec((1, G, S, HEAD_PAD), lambda b, h, i: (b, h, 0, 0)),
            pl.BlockSpec((1, G, MLA_VT_ROWS, S), lambda b, h, i: (b, h, 0, 0)),
            pl.BlockSpec((1, MLA_Q_TILE, G * MLA_V), lambda b, h, i: (b, i, h)),
        ],
        out_specs=pl.BlockSpec((1, MLA_Q_TILE, G * MLA_V), lambda b, h, i: (b, i, h)),
        scratch_shapes=[
            pltpu.VMEM((2, G, MLA_K_TILE, MLA_Q_TILE), F32),
            pltpu.VMEM((G, 1, MLA_Q_TILE), F32),
            pltpu.VMEM((G, MLA_VT_ROWS, MLA_Q_TILE), F32),
        ],
        compiler_params=_cparams(("parallel", "parallel", "parallel")),
        name="mla_attn",
    )(qt, k, vt, sg)


def _diff_pre_kernel(x_ref, mod_ref, g_ref, win_ref, qt_out, k_out, vt_out, sg_out, kn_out):
    h = _modulated_norm(x_ref, mod_ref, g_ref).astype(BF16)
    proj = _dot(h, win_ref[...])
    qscale = (DIFF_HD ** -0.5) * LOG2E
    sg_out[0] = _silu(proj[:, 3 * DIFF_W:])
    T = proj.shape[0]
    sub = lax.broadcasted_iota(jnp.int32, (BF16_SUBLANES, T), 0)
    ones_rows = (sub == 0).astype(F32)
    k_sq_max = []
    for hd in range(DIFF_HEADS):
        a = hd * HEAD_PAD
        b = a + HEAD_PAD
        qt_out[0, hd] = (proj[:, a:b] * qscale).T.astype(BF16)
        k_h = proj[:, DIFF_W + a:DIFF_W + b]
        k_out[0, hd] = k_h.astype(BF16)
        k_sq = jnp.sum(k_h * k_h, axis=-1, keepdims=True)
        k_sq_max.append(jnp.broadcast_to(jnp.max(k_sq, axis=0, keepdims=True), (1, LANES)))
        v_t = proj[:, 2 * DIFF_W + a:2 * DIFF_W + b].T
        vt_out[0, hd] = jnp.concatenate([v_t, ones_rows], axis=0).astype(BF16)
    kn_out[0, 0] = jnp.concatenate(k_sq_max, axis=0)


def _diff_pre(x, mod_i, g, win):
    B, S, _ = x.shape
    T = DIFF_PRE_ROW_TILE
    const = lambda shape: pl.BlockSpec(shape, lambda b, t: (0,) * len(shape))
    return pl.pallas_call(
        _diff_pre_kernel,
        out_shape=(
            jax.ShapeDtypeStruct((B, DIFF_HEADS, HEAD_PAD, S), BF16),
            jax.ShapeDtypeStruct((B, DIFF_HEADS, S, HEAD_PAD), BF16),
            jax.ShapeDtypeStruct((B, DIFF_HEADS, DIFF_VT_ROWS, S), BF16),
            jax.ShapeDtypeStruct((B, S, D_MODEL), F32),
            jax.ShapeDtypeStruct((B, S // T, DIFF_HEADS, LANES), F32),
        ),
        grid=(B, S // T),
        in_specs=[
            pl.BlockSpec((1, T, D_MODEL), lambda b, t: (b, t, 0)),
            pl.BlockSpec((1, 1, 3 * D_MODEL), lambda b, t: (b, 0, 0)),
            const((1, D_MODEL)),
            const(win.shape),
        ],
        out_specs=(
            pl.BlockSpec((1, DIFF_HEADS, HEAD_PAD, T), lambda b, t: (b, 0, 0, t)),
            pl.BlockSpec((1, DIFF_HEADS, T, HEAD_PAD), lambda b, t: (b, 0, t, 0)),
            pl.BlockSpec((1, DIFF_HEADS, DIFF_VT_ROWS, T), lambda b, t: (b, 0, 0, t)),
            pl.BlockSpec((1, T, D_MODEL), lambda b, t: (b, t, 0)),
            pl.BlockSpec((1, 1, DIFF_HEADS, LANES), lambda b, t: (b, t, 0, 0)),
        ),
        compiler_params=_cparams(("parallel", "parallel")),
        name="diff_pre",
    )(x, mod_i, g, win)


def _needed_trips(qt, posq, posrow_ref, kn_ref, slope, n_trips, keys_per_trip):
    qf = qt.astype(F32)
    q_sq = jnp.max(jnp.sum(qf * qf, axis=0, keepdims=True))
    score_bound = jnp.sqrt(q_sq * jnp.max(kn_ref[0])) * NORM_MARGIN
    limit = PRUNE_LOG2 + 2.0 * score_bound
    q_lo = jnp.min(posq)
    q_hi = jnp.max(posq)
    first = jnp.int32(n_trips - 1)
    last = jnp.int32(0)
    for p in range(n_trips):
        pk = posrow_ref[0, :, p * keys_per_trip:(p + 1) * keys_per_trip]
        gap = jnp.maximum(jnp.maximum(jnp.min(pk) - q_hi, q_lo - jnp.max(pk)), 0)
        needed = gap.astype(F32) * slope <= limit
        first = jnp.where(needed, jnp.minimum(first, p), first)
        last = jnp.where(needed, jnp.maximum(last, p), last)
    return first, jnp.maximum(last, first)


def _diff_attn_kernel(qt_ref, k_ref, vt_ref, sg_ref, posq_ref, posk_ref, posrow_ref, kn_ref,
                      slope_ref, lq1_ref, lk1_ref, lq2_ref, lk2_ref, hg_ref, o_ref,
                      s_ref, m_ref, acc_ref, *, lambda_init):
    n_kv = k_ref.shape[2] // DIFF_K_TILE
    m_ref[...] = jnp.full(m_ref.shape, -jnp.inf, F32)
    acc_ref[...] = jnp.zeros(acc_ref.shape, F32)

    qt = qt_ref[0, 0]
    row = lax.broadcasted_iota(jnp.int32, qt.shape, 0)
    zero = jnp.zeros_like(qt)
    qt_maps = (jnp.where(row < DIFF_HD, qt, zero), jnp.where(row >= DIFF_HD, qt, zero))
    slope = slope_ref[0][:, 0:1]
    pos_q = posq_ref[0].astype(F32) * slope

    def scores_fn(t):
        keys = _key_slice(t, DIFF_K_TILE)
        pos_k = posk_ref[0, keys, :].astype(F32) * slope
        bias = jnp.abs(pos_k - pos_q)
        kk = k_ref[0, 0, keys, :]
        return [lambda qt_c=qt_c: _dot(kk, qt_c) - bias for qt_c in qt_maps]

    def values_fn(t):
        vt = vt_ref[0, 0, :, _key_slice(t, DIFF_K_TILE)]
        return [vt, vt]

    assert n_kv % DIFF_STEPS_PER_TRIP == 0
    first_trip, last_trip = _needed_trips(
        qt, posq_ref[0], posrow_ref, kn_ref, jnp.max(slope_ref[0]),
        n_kv // DIFF_STEPS_PER_TRIP, DIFF_STEPS_PER_TRIP * DIFF_K_TILE)
    _pipelined_key_loop(scores_fn, values_fn, s_ref, m_ref, acc_ref,
                        steps_per_trip=DIFF_STEPS_PER_TRIP, first_trip=first_trip,
                        last_trip=last_trip)

    lam = (jnp.exp(jnp.sum(lq1_ref[...] * lk1_ref[...], axis=-1, keepdims=True))
           - jnp.exp(jnp.sum(lq2_ref[...] * lk2_ref[...], axis=-1, keepdims=True))
           + lambda_init)
    a1 = acc_ref[0]
    a2 = acc_ref[1]
    o = a1[:DIFF_V] / a1[DIFF_V:DIFF_V + 1] - lam * (a2[:DIFF_V] / a2[DIFF_V:DIFF_V + 1])
    o = o * lax.rsqrt(jnp.mean(o * o, axis=0, keepdims=True) + EPS)
    o = o.T * (hg_ref[...] * (1.0 - lambda_init))
    o_ref[0] = (o * sg_ref[0]).astype(BF16)


def _diff_attn(qt, k, vt, sg, pos_col, pos_row, k_sq_max, slopes, lq1, lk1, lq2, lk2, hg, lambda_init):
    B, H, S, _ = k.shape
    vec = lambda n: pl.BlockSpec((1, n), lambda b, h, i: (0, 0))
    return pl.pallas_call(
        functools.partial(_diff_attn_kernel, lambda_init=lambda_init),
        out_shape=jax.ShapeDtypeStruct((B, S, DIFF_W), BF16),
        grid=(B, H, S // DIFF_Q_TILE),
        in_specs=[
            pl.BlockSpec((1, 1, HEAD_PAD, DIFF_Q_TILE), lambda b, h, i: (b, h, 0, i)),
            pl.BlockSpec((1, 1, S, HEAD_PAD), lambda b, h, i: (b, h, 0, 0)),
            pl.BlockSpec((1, 1, DIFF_VT_ROWS, S), lambda b, h, i: (b, h, 0, 0)),
            pl.BlockSpec((1, DIFF_Q_TILE, HEAD_PAD), lambda b, h, i: (b, i, h)),
            pl.BlockSpec((1, 1, DIFF_Q_TILE), lambda b, h, i: (b, 0, i)),
            pl.BlockSpec((1, S, 1), lambda b, h, i: (b, 0, 0)),
            pl.BlockSpec((1, 1, S), lambda b, h, i: (b, 0, 0)),
            pl.BlockSpec((1,) + k_sq_max.shape[1:], lambda b, h, i: (b, 0, 0, 0)),
            pl.BlockSpec((1, 1, LANES), lambda b, h, i: (h, 0, 0)),
            vec(DIFF_HD), vec(DIFF_HD), vec(DIFF_HD), vec(DIFF_HD),
            vec(DIFF_V),
        ],
        out_specs=pl.BlockSpec((1, DIFF_Q_TILE, HEAD_PAD), lambda b, h, i: (b, i, h)),
        scratch_shapes=[
            pltpu.VMEM((2, 2, DIFF_K_TILE, DIFF_Q_TILE), F32),
            pltpu.VMEM((2, 1, DIFF_Q_TILE), F32),
            pltpu.VMEM((2, DIFF_VT_ROWS, DIFF_Q_TILE), F32),
        ],
        compiler_params=_cparams(("parallel", "parallel", "parallel")),
        name="diff_attn",
    )(qt, k, vt, sg, pos_row, pos_col, pos_row, k_sq_max, slopes, lq1, lk1, lq2, lk2, hg)


def _post_kernel(o_ref, x_ref, mod_ref, wo_ref, fg_ref, out_ref, *, final):
    y = _dot(o_ref[0], wo_ref[...])
    gate = mod_ref[0, :, 2 * D_MODEL:]
    x_new = x_ref[0] + gate * y
    if final:
        x_new = _rms(x_new, fg_ref[...])
    out_ref[0] = x_new


def _post(o, x, mod_i, wo, fg, final):
    B, S, _ = x.shape
    T = POST_ROW_TILE
    const = lambda shape: pl.BlockSpec(shape, lambda b, t: (0,) * len(shape))
    return pl.pallas_call(
        functools.partial(_post_kernel, final=final),
        out_shape=jax.ShapeDtypeStruct((B, S, D_MODEL), F32),
        grid=(B, S // T),
        in_specs=[
            pl.BlockSpec((1, T, D_MODEL), lambda b, t: (b, t, 0)),
            pl.BlockSpec((1, T, D_MODEL), lambda b, t: (b, t, 0)),
            pl.BlockSpec((1, 1, 3 * D_MODEL), lambda b, t: (b, 0, 0)),
            const(wo.shape),
            const((1, D_MODEL)),
        ],
        out_specs=pl.BlockSpec((1, T, D_MODEL), lambda b, t: (b, t, 0)),
        compiler_params=_cparams(("parallel", "parallel")),
        name="post_final" if final else "post",
    )(o, x, mod_i, wo, fg)


def _rot_cols(w):
    half = MLA_ROPE // 2
    return jnp.concatenate([-w[..., half:], w[..., :half]], axis=-1)


def _pad_slab(w, lo):
    pad = [(0, 0)] * (w.ndim - 1) + [(lo, HEAD_PAD - lo - w.shape[-1])]
    return jnp.pad(w, pad)


def _mla_weights(w_in, w_q_up, w_kv_up):
    o1 = MLA_Q_RANK
    o2 = o1 + MLA_KV_RANK
    o3 = o2 + MLA_ROPE
    w_kr = w_in[:, o2:o3]
    win = jnp.concatenate([
        w_in[:, :o2],
        _pad_slab(w_kr, MLA_NOPE),
        _pad_slab(_rot_cols(w_kr), MLA_NOPE),
        w_in[:, o3:],
    ], axis=1).astype(BF16)
    wq3 = w_q_up.reshape(MLA_Q_RANK, MLA_HEADS, MLA_QK)
    wq_plain = _pad_slab(wq3, 0).reshape(MLA_Q_RANK, MLA_HEADS * HEAD_PAD)
    wq_rot = _pad_slab(_rot_cols(wq3[..., MLA_NOPE:]), MLA_NOPE).reshape(MLA_Q_RANK, MLA_HEADS * HEAD_PAD)
    wq = jnp.concatenate([wq_plain, wq_rot], axis=1).astype(BF16)
    wkv3 = w_kv_up.reshape(MLA_KV_RANK, MLA_HEADS, MLA_NOPE + MLA_V)
    wk = _pad_slab(wkv3[..., :MLA_NOPE], 0).reshape(MLA_KV_RANK, MLA_HEADS * HEAD_PAD)
    wv = _pad_slab(wkv3[..., MLA_NOPE:], 0).reshape(MLA_KV_RANK, MLA_HEADS * HEAD_PAD)
    wkv = jnp.concatenate([wk, wv], axis=1).astype(BF16)
    return win, wq, wkv


def kernel(x, c, positions, ada_w, ada_b, norm_g, mla_w_in, mla_q_norm_g, mla_w_q_up, mla_kv_norm_g,
           mla_w_kv_up, mla_w_o, diff_w_in, diff_lq1, diff_lk1, diff_lq2, diff_lk2, diff_head_g,
           diff_w_o, final_g):
    B, S, D = x.shape
    tiles = (MLA_PRE_ROW_TILE, DIFF_PRE_ROW_TILE, POST_ROW_TILE, ROPE_ROW_TILE,
             MLA_Q_TILE, MLA_K_TILE, DIFF_Q_TILE, DIFF_K_TILE)
    assert D == D_MODEL and all(S % t == 0 for t in tiles)
    sublanes = 8
    assert B <= sublanes
    c_pad = jnp.pad(c, ((0, sublanes - B), (0, 0)))
    mod = _ada_mod(c_pad, ada_w, ada_b)[:, :B].reshape(DEPTH, B, 1, 3 * D)

    pos_col = positions.reshape(B, S, 1)
    pos_row = positions.reshape(B, 1, S)
    inv = ROPE_BASE ** (-jnp.arange(0, MLA_ROPE, 2, dtype=F32) / MLA_ROPE)
    cos, sin = _rope_tables(pos_row, jnp.concatenate([inv, inv]).reshape(MLA_ROPE, 1))
    slopes = jnp.exp2(-8.0 * jnp.arange(1, DIFF_HEADS + 1, dtype=F32) / DIFF_HEADS) * LOG2E
    slopes = jnp.broadcast_to(slopes[:, None, None], (DIFF_HEADS, 1, LANES))
    final_row = final_g.reshape(1, D)

    for i in range(DEPTH):
        j = i // 2
        g = norm_g[i].reshape(1, D)
        last = i == DEPTH - 1
        if i % 2 == 0:
            win, wq, wkv = _mla_weights(mla_w_in[j], mla_w_q_up[j], mla_w_kv_up[j])
            qt, k, vt, sg = _mla_pre(x, mod[i], g, cos, sin, win,
                                     mla_q_norm_g[j].reshape(1, -1), wq,
                                     mla_kv_norm_g[j].reshape(1, -1), wkv)
            o = _mla_attn(qt, k, vt, sg)
            wo = mla_w_o[j].astype(BF16)
        else:
            lambda_init = 0.8 - 0.6 * math.exp(-0.3 * i)
            qt, k, vt, sg, k_sq_max = _diff_pre(x, mod[i], g, diff_w_in[j].astype(BF16))
            o = _diff_attn(qt, k, vt, sg, pos_col, pos_row, k_sq_max, slopes,
                           diff_lq1[j].reshape(1, -1), diff_lk1[j].reshape(1, -1),
                           diff_lq2[j].reshape(1, -1), diff_lk2[j].reshape(1, -1),
                           diff_head_g[j].reshape(1, -1), lambda_init)
            wo = diff_w_o[j].astype(BF16)
        x = _post(o, x, mod[i], wo, final_row, last)
    return x
```

```python
import functools
import math

import jax
import jax.numpy as jnp
from jax import lax
from jax.experimental import pallas as pl
from jax.experimental.pallas import tpu as pltpu

D_MODEL = 1024
DEPTH = 4
EPS = 1e-6

MLA_HEADS = 16
MLA_NOPE = 64
MLA_ROPE = 32
MLA_V = 64
MLA_Q_RANK = 384
MLA_KV_RANK = 256
ROPE_BASE = 10000.0
MLA_QK = MLA_NOPE + MLA_ROPE

DIFF_HD = 64
DIFF_HEADS = D_MODEL // (2 * DIFF_HD)
DIFF_W = DIFF_HEADS * 2 * DIFF_HD
DIFF_V = 2 * DIFF_HD

LANES = 128
BF16_SUBLANES = 16
HEAD_PAD = LANES
LOG2E = math.log2(math.e)

MLA_VT_ROWS = MLA_V
DIFF_VT_ROWS = DIFF_V + BF16_SUBLANES

MLA_PRE_ROW_TILE = 256
DIFF_PRE_ROW_TILE = 512
POST_ROW_TILE = 1024
ROPE_ROW_TILE = 512
MLA_Q_TILE = 512
MLA_K_TILE = 512
DIFF_Q_TILE = 512
DIFF_K_TILE = 256
MLA_STEPS_PER_TRIP = 4
DIFF_STEPS_PER_TRIP = 2
PRUNE_LOG2 = 150.0
NORM_MARGIN = 1.02
VMEM_LIMIT = 56 * 1024 * 1024

BF16 = jnp.bfloat16
F32 = jnp.float32


def _cparams(sem):
    return pltpu.CompilerParams(dimension_semantics=sem, vmem_limit_bytes=VMEM_LIMIT)


def _dot(a, b):
    return jnp.dot(a, b, preferred_element_type=F32)


def _rms(x, g):
    return x * lax.rsqrt(jnp.mean(x * x, axis=-1, keepdims=True) + EPS) * g


def _silu(x):
    return x * (1.0 / (1.0 + jnp.exp(-x)))


def _mod_kernel(c_ref, w_ref, b_ref, o_ref):
    c = c_ref[...]
    o_ref[0] = _dot(_silu(c).astype(BF16), w_ref[0].astype(BF16)) + b_ref[0]


def _ada_mod(c_pad, ada_w, ada_b):
    rows = c_pad.shape[0]
    nblk = 3
    return pl.pallas_call(
        _mod_kernel,
        out_shape=jax.ShapeDtypeStruct((DEPTH, rows, 3 * D_MODEL), F32),
        grid=(DEPTH, nblk),
        in_specs=[
            pl.BlockSpec((rows, D_MODEL), lambda i, n: (0, 0)),
            pl.BlockSpec((1, D_MODEL, D_MODEL), lambda i, n: (i, 0, n)),
            pl.BlockSpec((1, 1, D_MODEL), lambda i, n: (i, 0, n)),
        ],
        out_specs=pl.BlockSpec((1, rows, D_MODEL), lambda i, n: (i, 0, n)),
        compiler_params=_cparams(("arbitrary", "arbitrary")),
        name="ada_mod",
    )(c_pad, ada_w, ada_b.reshape(DEPTH, 1, 3 * D_MODEL))


def _modulated_norm(x_ref, mod_ref, g_ref):
    x = x_ref[0]
    shift = mod_ref[0, :, 0:D_MODEL]
    scale = mod_ref[0, :, D_MODEL:2 * D_MODEL]
    return _rms(x, g_ref[...]) * (1.0 + scale) + shift


def _pipelined_key_loop(scores_fn, values_fn, s_ref, m_ref, acc_ref, *, steps_per_trip,
                        first_trip, last_trip, l_ref=None):
    def step(t_cons, slot, col_max, t_prod):
        thunks = scores_fn(t_prod) if t_prod is not None else None
        next_max = []
        for c, vt in enumerate(values_fn(t_cons)):
            if thunks is not None:
                s = thunks[c]()
                s_ref[1 - slot, c] = s
                next_max.append(jnp.max(s, axis=0, keepdims=True))
            m_prev = m_ref[c]
            m_new = jnp.maximum(m_prev, col_max[c])
            p32 = jnp.exp2(s_ref[slot, c] - m_new)
            p = p32.astype(BF16)
            alpha = jnp.exp2(m_prev - m_new)
            m_ref[c] = m_new
            if l_ref is not None:
                l_ref[c] = alpha * l_ref[c] + jnp.sum(p32, axis=0, keepdims=True)
            acc_ref[c] = alpha * acc_ref[c] + _dot(vt, p)
        return tuple(next_max)

    def trip(i, col_max, last):
        t0 = steps_per_trip * i
        for u in range(steps_per_trip):
            final_step = last and u == steps_per_trip - 1
            col_max = step(t0 + u, u % 2, col_max, None if final_step else t0 + u + 1)
        return col_max

    assert steps_per_trip % 2 == 0
    cm = []
    for c, thunk in enumerate(scores_fn(steps_per_trip * first_trip)):
        s = thunk()
        s_ref[0, c] = s
        cm.append(jnp.max(s, axis=0, keepdims=True))
    cm = lax.fori_loop(first_trip, last_trip, lambda i, cm: trip(i, cm, False), tuple(cm))
    trip(last_trip, cm, True)


def _rope_kernel(pos_ref, inv_ref, cos_out, sin_out):
    ang = inv_ref[...] * pos_ref[0].astype(F32)
    T = ang.shape[1]
    ones = jnp.ones((MLA_NOPE, T), F32)
    tail = HEAD_PAD - MLA_QK
    cos_out[0] = jnp.concatenate([ones, jnp.cos(ang), ones[:tail]], axis=0).T
    sin_out[0] = jnp.concatenate([0.0 * ones, jnp.sin(ang), 0.0 * ones[:tail]], axis=0).T


def _rope_tables(pos_row, inv_col):
    B, _, S = pos_row.shape
    T = ROPE_ROW_TILE
    table = jax.ShapeDtypeStruct((B, S, HEAD_PAD), F32)
    return pl.pallas_call(
        _rope_kernel,
        out_shape=(table, table),
        grid=(B, S // T),
        in_specs=[
            pl.BlockSpec((1, 1, T), lambda b, t: (b, 0, t)),
            pl.BlockSpec((MLA_ROPE, 1), lambda b, t: (0, 0)),
        ],
        out_specs=(
            pl.BlockSpec((1, T, HEAD_PAD), lambda b, t: (b, t, 0)),
            pl.BlockSpec((1, T, HEAD_PAD), lambda b, t: (b, t, 0)),
        ),
        compiler_params=_cparams(("parallel", "parallel")),
        name="rope_tables",
    )(pos_row, inv_col)


def _mla_pre_kernel(x_ref, mod_ref, g_ref, cos_ref, sin_ref, win_ref, qg_ref, wq_ref,
                    kvg_ref, wkv_ref, qt_out, k_out, vt_out, sg_out):
    h = _modulated_norm(x_ref, mod_ref, g_ref).astype(BF16)
    proj = _dot(h, win_ref[...])
    o1 = MLA_Q_RANK
    o2 = o1 + MLA_KV_RANK
    o3 = o2 + HEAD_PAD
    o4 = o3 + HEAD_PAD
    qn = _rms(proj[:, :o1], qg_ref[...]).astype(BF16)
    kvn = _rms(proj[:, o1:o2], kvg_ref[...]).astype(BF16)
    cos = cos_ref[0]
    sin = sin_ref[0]
    k_rope = proj[:, o2:o3] * cos + proj[:, o3:o4] * sin
    sg_out[0] = _silu(proj[:, o4:])

    qq = _dot(qn, wq_ref[...])
    kv = _dot(kvn, wkv_ref[...])
    half = MLA_HEADS * HEAD_PAD
    qscale = (MLA_QK ** -0.5) * LOG2E
    for hd in range(MLA_HEADS):
        a = hd * HEAD_PAD
        b = a + HEAD_PAD
        q_h = (qq[:, a:b] * cos + qq[:, half + a:half + b] * sin) * qscale
        qt_out[0, hd] = q_h.T.astype(BF16)
        k_out[0, hd] = (kv[:, a:b] + k_rope).astype(BF16)
        vt_out[0, hd] = kv[:, half + a:half + b].T[:MLA_VT_ROWS].astype(BF16)


def _mla_pre(x, mod_i, g, cos, sin, win, qg, wq, kvg, wkv):
    B, S, _ = x.shape
    T = MLA_PRE_ROW_TILE
    const = lambda shape: pl.BlockSpec(shape, lambda b, t: (0,) * len(shape))
    return pl.pallas_call(
        _mla_pre_kernel,
        out_shape=(
            jax.ShapeDtypeStruct((B, MLA_HEADS, HEAD_PAD, S), BF16),
            jax.ShapeDtypeStruct((B, MLA_HEADS, S, HEAD_PAD), BF16),
            jax.ShapeDtypeStruct((B, MLA_HEADS, MLA_VT_ROWS, S), BF16),
            jax.ShapeDtypeStruct((B, S, D_MODEL), F32),
        ),
        grid=(B, S // T),
        in_specs=[
            pl.BlockSpec((1, T, D_MODEL), lambda b, t: (b, t, 0)),
            pl.BlockSpec((1, 1, 3 * D_MODEL), lambda b, t: (b, 0, 0)),
            const((1, D_MODEL)),
            pl.BlockSpec((1, T, HEAD_PAD), lambda b, t: (b, t, 0)),
            pl.BlockSpec((1, T, HEAD_PAD), lambda b, t: (b, t, 0)),
            const(win.shape),
            const((1, MLA_Q_RANK)),
            const(wq.shape),
            const((1, MLA_KV_RANK)),
            const(wkv.shape),
        ],
        out_specs=(
            pl.BlockSpec((1, MLA_HEADS, HEAD_PAD, T), lambda b, t: (b, 0, 0, t)),
            pl.BlockSpec((1, MLA_HEADS, T, HEAD_PAD), lambda b, t: (b, 0, t, 0)),
            pl.BlockSpec((1, MLA_HEADS, MLA_VT_ROWS, T), lambda b, t: (b, 0, 0, t)),
            pl.BlockSpec((1, T, D_MODEL), lambda b, t: (b, t, 0)),
        ),
        compiler_params=_cparams(("parallel", "parallel")),
        name="mla_pre",
    )(x, mod_i, g, cos, sin, win, qg, wq, kvg, wkv)


MLA_HEADS_PER_STEP = 4


def _key_slice(t, tile):
    return pl.ds(pl.multiple_of(t * tile, tile), tile)


def _key_scores(k_ref, qt_ref, hh, t):
    return _dot(k_ref[0, hh, _key_slice(t, MLA_K_TILE), :], qt_ref[0, hh])


def _mla_attn_kernel(qt_ref, k_ref, vt_ref, sg_ref, o_ref, s_ref, m_ref, l_ref, acc_ref):
    n_kv = k_ref.shape[2] // MLA_K_TILE
    m_ref[...] = jnp.full(m_ref.shape, -jnp.inf, F32)
    l_ref[...] = jnp.zeros(l_ref.shape, F32)
    acc_ref[...] = jnp.zeros(acc_ref.shape, F32)

    heads = range(MLA_HEADS_PER_STEP)

    def scores_fn(t):
        return [functools.partial(_key_scores, k_ref, qt_ref, hh, t) for hh in heads]

    def values_fn(t):
        return [vt_ref[0, hh, :, _key_slice(t, MLA_K_TILE)] for hh in heads]

    assert n_kv % MLA_STEPS_PER_TRIP == 0
    _pipelined_key_loop(scores_fn, values_fn, s_ref, m_ref, acc_ref,
                        steps_per_trip=MLA_STEPS_PER_TRIP, first_trip=0,
                        last_trip=n_kv // MLA_STEPS_PER_TRIP - 1, l_ref=l_ref)

    outs = []
    for hh in range(MLA_HEADS_PER_STEP):
        outs.append(acc_ref[hh] / l_ref[hh])
    o = jnp.concatenate(outs, axis=0).T
    o_ref[0] = (o * sg_ref[0]).astype(BF16)


def _mla_attn(qt, k, vt, sg):
    B, H, S, _ = k.shape
    G = MLA_HEADS_PER_STEP
    return pl.pallas_call(
        _mla_attn_kernel,
        out_shape=jax.ShapeDtypeStruct((B, S, H * MLA_V), BF16),
        grid=(B, H // G, S // MLA_Q_TILE),
        in_specs=[
            pl.BlockSpec((1, G, HEAD_PAD, MLA_Q_TILE), lambda b, h, i: (b, h, 0, i)),
            pl.BlockSpec((1, G, S, HEAD_PAD), lambda b, h, i: (b, h, 0, 0)),
            pl.BlockSpec((1, G, MLA_VT_ROWS, S), lambda b, h, i: (b, h, 0, 0)),
            pl.BlockSpec((1, MLA_Q_TILE, G * MLA_V), lambda b, h, i: (b, i, h)),
        ],
        out_specs=pl.BlockSpec((1, MLA_Q_TILE, G * MLA_V), lambda b, h, i: (b, i, h)),
        scratch_shapes=[
            pltpu.VMEM((2, G, MLA_K_TILE, MLA_Q_TILE), F32),
            pltpu.VMEM((G, 1, MLA_Q_TILE), F32),
            pltpu.VMEM((G, 1, MLA_Q_TILE), F32),
            pltpu.VMEM((G, MLA_VT_ROWS, MLA_Q_TILE), F32),
        ],
        compiler_params=_cparams(("parallel", "parallel", "parallel")),
        name="mla_attn",
    )(qt, k, vt, sg)


def _diff_pre_kernel(x_ref, mod_ref, g_ref, win_ref, qt_out, k_out, vt_out, sg_out, kn_out):
    h = _modulated_norm(x_ref, mod_ref, g_ref).astype(BF16)
    proj = _dot(h, win_ref[...])
    qscale = (DIFF_HD ** -0.5) * LOG2E
    sg_out[0] = _silu(proj[:, 3 * DIFF_W:])
    T = proj.shape[0]
    sub = lax.broadcasted_iota(jnp.int32, (BF16_SUBLANES, T), 0)
    ones_rows = (sub == 0).astype(F32)
    k_sq_max = []
    for hd in range(DIFF_HEADS):
        a = hd * HEAD_PAD
        b = a + HEAD_PAD
        qt_out[0, hd] = (proj[:, a:b] * qscale).T.astype(BF16)
        k_h = proj[:, DIFF_W + a:DIFF_W + b]
        k_out[0, hd] = k_h.astype(BF16)
        k_sq = jnp.sum(k_h * k_h, axis=-1, keepdims=True)
        k_sq_max.append(jnp.broadcast_to(jnp.max(k_sq, axis=0, keepdims=True), (1, LANES)))
        v_t = proj[:, 2 * DIFF_W + a:2 * DIFF_W + b].T
        vt_out[0, hd] = jnp.concatenate([v_t, ones_rows], axis=0).astype(BF16)
    kn_out[0, 0] = jnp.concatenate(k_sq_max, axis=0)


def _diff_pre(x, mod_i, g, win):
    B, S, _ = x.shape
    T = DIFF_PRE_ROW_TILE
    const = lambda shape: pl.BlockSpec(shape, lambda b, t: (0,) * len(shape))
    return pl.pallas_call(
        _diff_pre_kernel,
        out_shape=(
            jax.ShapeDtypeStruct((B, DIFF_HEADS, HEAD_PAD, S), BF16),
            jax.ShapeDtypeStruct((B, DIFF_HEADS, S, HEAD_PAD), BF16),
            jax.ShapeDtypeStruct((B, DIFF_HEADS, DIFF_VT_ROWS, S), BF16),
            jax.ShapeDtypeStruct((B, S, D_MODEL), F32),
            jax.ShapeDtypeStruct((B, S // T, DIFF_HEADS, LANES), F32),
        ),
        grid=(B, S // T),
        in_specs=[
            pl.BlockSpec((1, T, D_MODEL), lambda b, t: (b, t, 0)),
            pl.BlockSpec((1, 1, 3 * D_MODEL), lambda b, t: (b, 0, 0)),
            const((1, D_MODEL)),
            const(win.shape),
        ],
        out_specs=(
            pl.BlockSpec((1, DIFF_HEADS, HEAD_PAD, T), lambda b, t: (b, 0, 0, t)),
            pl.BlockSpec((1, DIFF_HEADS, T, HEAD_PAD), lambda b, t: (b, 0, t, 0)),
            pl.BlockSpec((1, DIFF_HEADS, DIFF_VT_ROWS, T), lambda b, t: (b, 0, 0, t)),
            pl.BlockSpec((1, T, D_MODEL), lambda b, t: (b, t, 0)),
            pl.BlockSpec((1, 1, DIFF_HEADS, LANES), lambda b, t: (b, t, 0, 0)),
        ),
        compiler_params=_cparams(("parallel", "parallel")),
        name="diff_pre",
    )(x, mod_i, g, win)


def _needed_trips(qt, posq, posrow_ref, kn_ref, slope, n_trips, keys_per_trip):
    qf = qt.astype(F32)
    q_sq = jnp.max(jnp.sum(qf * qf, axis=0, keepdims=True))
    score_bound = jnp.sqrt(q_sq * jnp.max(kn_ref[0])) * NORM_MARGIN
    limit = PRUNE_LOG2 + 2.0 * score_bound
    q_lo = jnp.min(posq)
    q_hi = jnp.max(posq)
    first = jnp.int32(n_trips - 1)
    last = jnp.int32(0)
    for p in range(n_trips):
        pk = posrow_ref[0, :, p * keys_per_trip:(p + 1) * keys_per_trip]
        gap = jnp.maximum(jnp.maximum(jnp.min(pk) - q_hi, q_lo - jnp.max(pk)), 0)
        needed = gap.astype(F32) * slope <= limit
        first = jnp.where(needed, jnp.minimum(first, p), first)
        last = jnp.where(needed, jnp.maximum(last, p), last)
    return first, jnp.maximum(last, first)


def _diff_attn_kernel(qt_ref, k_ref, vt_ref, sg_ref, posq_ref, posk_ref, posrow_ref, kn_ref,
                      slope_ref, lq1_ref, lk1_ref, lq2_ref, lk2_ref, hg_ref, o_ref,
                      s_ref, m_ref, acc_ref, *, lambda_init):
    n_kv = k_ref.shape[2] // DIFF_K_TILE
    m_ref[...] = jnp.full(m_ref.shape, -jnp.inf, F32)
    acc_ref[...] = jnp.zeros(acc_ref.shape, F32)

    qt = qt_ref[0, 0]
    row = lax.broadcasted_iota(jnp.int32, qt.shape, 0)
    zero = jnp.zeros_like(qt)
    qt_maps = (jnp.where(row < DIFF_HD, qt, zero), jnp.where(row >= DIFF_HD, qt, zero))
    slope = slope_ref[0][:, 0:1]
    pos_q = posq_ref[0].astype(F32) * slope

    def scores_fn(t):
        keys = _key_slice(t, DIFF_K_TILE)
        pos_k = posk_ref[0, keys, :].astype(F32) * slope
        bias = jnp.abs(pos_k - pos_q)
        kk = k_ref[0, 0, keys, :]
        return [lambda qt_c=qt_c: _dot(kk, qt_c) - bias for qt_c in qt_maps]

    def values_fn(t):
        vt = vt_ref[0, 0, :, _key_slice(t, DIFF_K_TILE)]
        return [vt, vt]

    assert n_kv % DIFF_STEPS_PER_TRIP == 0
    first_trip, last_trip = _needed_trips(
        qt, posq_ref[0], posrow_ref, kn_ref, jnp.max(slope_ref[0]),
        n_kv // DIFF_STEPS_PER_TRIP, DIFF_STEPS_PER_TRIP * DIFF_K_TILE)
    _pipelined_key_loop(scores_fn, values_fn, s_ref, m_ref, acc_ref,
                        steps_per_trip=DIFF_STEPS_PER_TRIP, first_trip=first_trip,
                        last_trip=last_trip)

    lam = (jnp.exp(jnp.sum(lq1_ref[...] * lk1_ref[...], axis=-1, keepdims=True))
           - jnp.exp(jnp.sum(lq2_ref[...] * lk2_ref[...], axis=-1, keepdims=True))
           + lambda_init)
    a1 = acc_ref[0]
    a2 = acc_ref[1]
    o = a1[:DIFF_V] / a1[DIFF_V:DIFF_V + 1] - lam * (a2[:DIFF_V] / a2[DIFF_V:DIFF_V + 1])
    o = o * lax.rsqrt(jnp.mean(o * o, axis=0, keepdims=True) + EPS)
    o = o.T * (hg_ref[...] * (1.0 - lambda_init))
    o_ref[0] = (o * sg_ref[0]).astype(BF16)


def _diff_attn(qt, k, vt, sg, pos_col, pos_row, k_sq_max, slopes, lq1, lk1, lq2, lk2, hg, lambda_init):
    B, H, S, _ = k.shape
    vec = lambda n: pl.BlockSpec((1, n), lambda b, h, i: (0, 0))
    return pl.pallas_call(
        functools.partial(_diff_attn_kernel, lambda_init=lambda_init),
        out_shape=jax.ShapeDtypeStruct((B, S, DIFF_W), BF16),
        grid=(B, H, S // DIFF_Q_TILE),
        in_specs=[
            pl.BlockSpec((1, 1, HEAD_PAD, DIFF_Q_TILE), lambda b, h, i: (b, h, 0, i)),
            pl.BlockSpec((1, 1, S, HEAD_PAD), lambda b, h, i: (b, h, 0, 0)),
            pl.BlockSpec((1, 1, DIFF_VT_ROWS, S), lambda b, h, i: (b, h, 0, 0)),
            pl.BlockSpec((1, DIFF_Q_TILE, HEAD_PAD), lambda b, h, i: (b, i, h)),
            pl.BlockSpec((1, 1, DIFF_Q_TILE), lambda b, h, i: (b, 0, i)),
            pl.BlockSpec((1, S, 1), lambda b, h, i: (b, 0, 0)),
            pl.BlockSpec((1, 1, S), lambda b, h, i: (b, 0, 0)),
            pl.BlockSpec((1,) + k_sq_max.shape[1:], lambda b, h, i: (b, 0, 0, 0)),
            pl.BlockSpec((1, 1, LANES), lambda b, h, i: (h, 0, 0)),
            vec(DIFF_HD), vec(DIFF_HD), vec(DIFF_HD), vec(DIFF_HD),
            vec(DIFF_V),
        ],
        out_specs=pl.BlockSpec((1, DIFF_Q_TILE, HEAD_PAD), lambda b, h, i: (b, i, h)),
        scratch_shapes=[
            pltpu.VMEM((2, 2, DIFF_K_TILE, DIFF_Q_TILE), F32),
            pltpu.VMEM((2, 1, DIFF_Q_TILE), F32),
            pltpu.VMEM((2, DIFF_VT_ROWS, DIFF_Q_TILE), F32),
        ],
        compiler_params=_cparams(("parallel", "parallel", "parallel")),
        name="diff_attn",
    )(qt, k, vt, sg, pos_row, pos_col, pos_row, k_sq_max, slopes, lq1, lk1, lq2, lk2, hg)


def _post_kernel(o_ref, x_ref, mod_ref, wo_ref, fg_ref, out_ref, *, final):
    y = _dot(o_ref[0], wo_ref[...])
    gate = mod_ref[0, :, 2 * D_MODEL:]
    x_new = x_ref[0] + gate * y
    if final:
        x_new = _rms(x_new, fg_ref[...])
    out_ref[0] = x_new


def _post(o, x, mod_i, wo, fg, final):
    B, S, _ = x.shape
    T = POST_ROW_TILE
    const = lambda shape: pl.BlockSpec(shape, lambda b, t: (0,) * len(shape))
    return pl.pallas_call(
        functools.partial(_post_kernel, final=final),
        out_shape=jax.ShapeDtypeStruct((B, S, D_MODEL), F32),
        grid=(B, S // T),
        in_specs=[
            pl.BlockSpec((1, T, D_MODEL), lambda b, t: (b, t, 0)),
            pl.BlockSpec((1, T, D_MODEL), lambda b, t: (b, t, 0)),
            pl.BlockSpec((1, 1, 3 * D_MODEL), lambda b, t: (b, 0, 0)),
            const(wo.shape),
            const((1, D_MODEL)),
        ],
        out_specs=pl.BlockSpec((1, T, D_MODEL), lambda b, t: (b, t, 0)),
        compiler_params=_cparams(("parallel", "parallel")),
        name="post_final" if final else "post",
    )(o, x, mod_i, wo, fg)


def _rot_cols(w):
    half = MLA_ROPE // 2
    return jnp.concatenate([-w[..., half:], w[..., :half]], axis=-1)


def _pad_slab(w, lo):
    pad = [(0, 0)] * (w.ndim - 1) + [(lo, HEAD_PAD - lo - w.shape[-1])]
    return jnp.pad(w, pad)


def _mla_weights(w_in, w_q_up, w_kv_up):
    o1 = MLA_Q_RANK
    o2 = o1 + MLA_KV_RANK
    o3 = o2 + MLA_ROPE
    w_kr = w_in[:, o2:o3]
    win = jnp.concatenate([
        w_in[:, :o2],
        _pad_slab(w_kr, MLA_NOPE),
        _pad_slab(_rot_cols(w_kr), MLA_NOPE),
        w_in[:, o3:],
    ], axis=1).astype(BF16)
    wq3 = w_q_up.reshape(MLA_Q_RANK, MLA_HEADS, MLA_QK)
    wq_plain = _pad_slab(wq3, 0).reshape(MLA_Q_RANK, MLA_HEADS * HEAD_PAD)
    wq_rot = _pad_slab(_rot_cols(wq3[..., MLA_NOPE:]), MLA_NOPE).reshape(MLA_Q_RANK, MLA_HEADS * HEAD_PAD)
    wq = jnp.concatenate([wq_plain, wq_rot], axis=1).astype(BF16)
    wkv3 = w_kv_up.reshape(MLA_KV_RANK, MLA_HEADS, MLA_NOPE + MLA_V)
    wk = _pad_slab(wkv3[..., :MLA_NOPE], 0).reshape(MLA_KV_RANK, MLA_HEADS * HEAD_PAD)
    wv = _pad_slab(wkv3[..., MLA_NOPE:], 0).reshape(MLA_KV_RANK, MLA_HEADS * HEAD_PAD)
    wkv = jnp.concatenate([wk, wv], axis=1).astype(BF16)
    return win, wq, wkv


def kernel(x, c, positions, ada_w, ada_b, norm_g, mla_w_in, mla_q_norm_g, mla_w_q_up, mla_kv_norm_g,
           mla_w_kv_up, mla_w_o, diff_w_in, diff_lq1, diff_lk1, diff_lq2, diff_lk2, diff_head_g,
           diff_w_o, final_g):
    B, S, D = x.shape
    tiles = (MLA_PRE_ROW_TILE, DIFF_PRE_ROW_TILE, POST_ROW_TILE, ROPE_ROW_TILE,
             MLA_Q_TILE, MLA_K_TILE, DIFF_Q_TILE, DIFF_K_TILE)
    assert D == D_MODEL and all(S % t == 0 for t in tiles)
    sublanes = 8
    assert B <= sublanes
    c_pad = jnp.pad(c, ((0, sublanes - B), (0, 0)))
    mod = _ada_mod(c_pad, ada_w, ada_b)[:, :B].reshape(DEPTH, B, 1, 3 * D)

    pos_col = positions.reshape(B, S, 1)
    pos_row = positions.reshape(B, 1, S)
    inv = ROPE_BASE ** (-jnp.arange(0, MLA_ROPE, 2, dtype=F32) / MLA_ROPE)
    cos, sin = _rope_tables(pos_row, jnp.concatenate([inv, inv]).reshape(MLA_ROPE, 1))
    slopes = jnp.exp2(-8.0 * jnp.arange(1, DIFF_HEADS + 1, dtype=F32) / DIFF_HEADS) * LOG2E
    slopes = jnp.broadcast_to(slopes[:, None, None], (DIFF_HEADS, 1, LANES))
    final_row = final_g.reshape(1, D)

    for i in range(DEPTH):
        j = i // 2
        g = norm_g[i].reshape(1, D)
        last = i == DEPTH - 1
        if i % 2 == 0:
            win, wq, wkv = _mla_weights(mla_w_in[j], mla_w_q_up[j], mla_w_kv_up[j])
            qt, k, vt, sg = _mla_pre(x, mod[i], g, cos, sin, win,
                                     mla_q_norm_g[j].reshape(1, -1), wq,
                                     mla_kv_norm_g[j].reshape(1, -1), wkv)
            o = _mla_attn(qt, k, vt, sg)
            wo = mla_w_o[j].astype(BF16)
        else:
            lambda_init = 0.8 - 0.6 * math.exp(-0.3 * i)
            qt, k, vt, sg, k_sq_max = _diff_pre(x, mod[i], g, diff_w_in[j].astype(BF16))
            o = _diff_attn(qt, k, vt, sg, pos_col, pos_row, k_sq_max, slopes,
                           diff_lq1[j].reshape(1, -1), diff_lk1[j].reshape(1, -1),
                           diff_lq2[j].reshape(1, -1), diff_lk2[j].reshape(1, -1),
                           diff_head_g[j].reshape(1, -1), lambda_init)
            wo = diff_w_o[j].astype(BF16)
        x = _post(o, x, mod[i], wo, final_row, last)
    return x
```
